```python
import jax, jax.numpy as jnp
from jax import lax
import numpy as np

D_MODEL = 1024
BATCH = 1
SEQ = 16384
DEPTH = 1

GRID_W = 64
CTX_LEN = 256
D_MIX = D_MODEL
D_CONV = D_MIX // 2
D_SSM = D_MIX - D_CONV
SSM_GROUP = 16
N_GROUPS = D_SSM // SSM_GROUP
STATE = 64
CONV_WIDTH = 31
D_FF = ((8 * D_MODEL + 3 * 256 - 1) // (3 * 256)) * 256
D_IN = 2 * D_CONV + D_SSM
N_MOD = 6
EPS = 1e-6

kernel_name = "hybrid_conformer_conv_s5_prefix_dit_block"


def rms_norm(x, g):
    xf = x.astype(jnp.float32)
    y = xf * lax.rsqrt(jnp.mean(xf * xf, axis=-1, keepdims=True) + EPS)
    return (y * g.astype(jnp.float32)).astype(x.dtype)


def layer_norm(x, g, b):
    xf = x.astype(jnp.float32)
    mu = jnp.mean(xf, axis=-1, keepdims=True)
    var = jnp.mean(jnp.square(xf - mu), axis=-1, keepdims=True)
    y = (xf - mu) * lax.rsqrt(var + EPS) * g.astype(jnp.float32) + b.astype(jnp.float32)
    return y.astype(x.dtype)


def modulate(h, shift, scale):
    return h * (1.0 + scale) + shift


def conv_module(a, w, b, ln_g, ln_b):
    pad = CONV_WIDTH // 2
    y = lax.conv_general_dilated(
        a, w[:, None, :].astype(a.dtype), window_strides=(1,), padding=((pad, pad),),
        dimension_numbers=("NWC", "WIO", "NWC"), feature_group_count=a.shape[-1])
    y = y + b
    return jax.nn.silu(layer_norm(y, ln_g, ln_b))


def glu(z):
    v, g = jnp.split(z, 2, axis=-1)
    return v * jax.nn.sigmoid(g)


def s5_discretize(a_re, a_im, log_dt, b_re, b_im):
    A = lax.complex(a_re.astype(jnp.float32), a_im.astype(jnp.float32))
    dt = jnp.exp(log_dt.astype(jnp.float32))[:, None]
    a_bar = jnp.exp(A * dt)
    Bm = lax.complex(b_re.astype(jnp.float32), b_im.astype(jnp.float32))
    b_bar = ((a_bar - 1.0) / A)[:, :, None] * Bm
    return a_bar, b_bar


def linear_scan(a_bar, bu, s0, reverse):
    if s0 is not None:
        first = -1 if reverse else 0
        bu = bu.at[:, first].add(a_bar * s0)
    a = jnp.broadcast_to(a_bar, bu.shape)

    def combine(e1, e2):
        a1, b1 = e1
        a2, b2 = e2
        return a1 * a2, a2 * b1 + b2

    _, states = lax.associative_scan(combine, (a, bu), reverse=reverse, axis=1)
    return states


def s5_block(u_ctx, u_lat, a_re, a_im, log_dt, b_re, b_im, c_re, c_im, d_skip, w_glu, b_glu,
             need_ctx_out):
    B_, L, _ = u_lat.shape
    uc = u_ctx.astype(jnp.float32)
    ul = u_lat.astype(jnp.float32)
    ucg = uc.reshape(B_, uc.shape[1], N_GROUPS, SSM_GROUP)
    ulg = ul.reshape(B_, L, N_GROUPS, SSM_GROUP)
    d = d_skip.astype(jnp.float32).reshape(N_GROUPS, SSM_GROUP)
    y_lat = ulg * d
    y_ctx = ucg * d
    for direction, reverse in ((0, False), (1, True)):
        a_bar, b_bar = s5_discretize(a_re[direction], a_im[direction], log_dt[direction],
                                     b_re[direction], b_im[direction])
        c_mat = lax.complex(c_re[direction].astype(jnp.float32),
                            c_im[direction].astype(jnp.float32))
        s_ctx = linear_scan(a_bar, jnp.einsum('gpc,blgc->blgp', b_bar, ucg), None, reverse)
        s_final = s_ctx[:, 0] if reverse else s_ctx[:, -1]
        s_lat = linear_scan(a_bar, jnp.einsum('gpc,blgc->blgp', b_bar, ulg), s_final, reverse)
        y_lat = y_lat + jnp.einsum('blgp,gcp->blgc', s_lat, c_mat).real
        if need_ctx_out:
            y_ctx = y_ctx + jnp.einsum('blgp,gcp->blgc', s_ctx, c_mat).real

    def out_glu(y):
        g = jax.nn.gelu(y.reshape(y.shape[0], y.shape[1], D_SSM))
        return g * jax.nn.sigmoid(g @ w_glu.astype(jnp.float32) + b_glu.astype(jnp.float32))

    out_lat = out_glu(y_lat).astype(u_lat.dtype)
    out_ctx = out_glu(y_ctx).astype(u_ctx.dtype) if need_ctx_out else None
    return out_lat, out_ctx


def swiglu(h, w_gate_up, w_down):
    g, u = jnp.split(h @ w_gate_up, 2, axis=-1)
    return (jax.nn.silu(g) * u) @ w_down


def setup_inputs(seed: int = 0) -> dict:
    key = jax.random.key(seed)
    ks = jax.random.split(key, 32)
    f32 = jnp.float32
    nrm = lambda k, shape, s: jax.random.normal(k, shape, f32) * s
    n_idx = jnp.arange(STATE, dtype=f32)
    return {
        "x": nrm(ks[0], (BATCH, SEQ, D_MODEL), 1.0),
        "c": nrm(ks[1], (BATCH, D_MODEL), 1.0),
        "ctx": nrm(ks[2], (BATCH, CTX_LEN, D_MODEL), 1.0),
        "c_ctx": nrm(ks[3], (D_MODEL,), 1.0),
        "w_mod": nrm(ks[4], (DEPTH, D_MODEL, N_MOD * D_MODEL), 0.5 * D_MODEL ** -0.5),
        "b_mod": nrm(ks[5], (DEPTH, N_MOD * D_MODEL), 0.01),
        "norm1_g": 1.0 + nrm(ks[6], (DEPTH, D_MODEL), 0.01),
        "w_in": nrm(ks[7], (DEPTH, D_MODEL, D_IN), D_MODEL ** -0.5),
        "conv_w": nrm(ks[8], (DEPTH, CONV_WIDTH, D_CONV), CONV_WIDTH ** -0.5),
        "conv_b": nrm(ks[9], (DEPTH, D_CONV), 0.01),
        "conv_ln_g": 1.0 + nrm(ks[10], (DEPTH, D_CONV), 0.01),
        "conv_ln_b": nrm(ks[11], (DEPTH, D_CONV), 0.01),
        "ssm_a_re": -0.5 * jnp.exp(nrm(ks[12], (DEPTH, 2, N_GROUPS, STATE), 0.01)),
        "ssm_a_im": np.pi * n_idx + nrm(ks[13], (DEPTH, 2, N_GROUPS, STATE), 0.01),
        "ssm_log_dt": jax.random.uniform(ks[14], (DEPTH, 2, N_GROUPS), f32,
                                         np.log(0.001), np.log(0.1)),
        "ssm_b_re": nrm(ks[15], (DEPTH, 2, N_GROUPS, STATE, SSM_GROUP), (2 * SSM_GROUP) ** -0.5),
        "ssm_b_im": nrm(ks[16], (DEPTH, 2, N_GROUPS, STATE, SSM_GROUP), (2 * SSM_GROUP) ** -0.5),
        "ssm_c_re": nrm(ks[17], (DEPTH, 2, N_GROUPS, SSM_GROUP, STATE), STATE ** -0.5),
        "ssm_c_im": nrm(ks[18], (DEPTH, 2, N_GROUPS, SSM_GROUP, STATE), STATE ** -0.5),
        "ssm_d": nrm(ks[19], (DEPTH, D_SSM), 1.0),
        "ssm_w_glu": nrm(ks[20], (DEPTH, D_SSM, D_SSM), D_SSM ** -0.5),
        "ssm_b_glu": nrm(ks[21], (DEPTH, D_SSM), 0.01),
        "w_out": nrm(ks[22], (DEPTH, D_MIX, D_MODEL), D_MIX ** -0.5),
        "norm2_g": 1.0 + nrm(ks[23], (DEPTH, D_MODEL), 0.01),
        "w_gate_up": nrm(ks[24], (DEPTH, D_MODEL, 2 * D_FF), D_MODEL ** -0.5),
        "w_down": nrm(ks[25], (DEPTH, D_FF, D_MODEL), D_FF ** -0.5),
        "final_norm_g": 1.0 + nrm(ks[26], (D_MODEL,), 0.01),
    }


def reference(x, c, ctx, c_ctx, w_mod, b_mod, norm1_g, w_in, conv_w, conv_b, conv_ln_g,
              conv_ln_b, ssm_a_re, ssm_a_im, ssm_log_dt, ssm_b_re, ssm_b_im, ssm_c_re,
              ssm_c_im, ssm_d, ssm_w_glu, ssm_b_glu, w_out, norm2_g, w_gate_up, w_down,
              final_norm_g):
    B_, L, _ = x.shape
    rows = L // GRID_W
    silu_c = jax.nn.silu(c)
    silu_cc = jax.nn.silu(c_ctx)
    for i in range(DEPTH):
        need_ctx = i < DEPTH - 1
        sh1, sc1, g1, sh2, sc2, g2 = jnp.split(silu_c @ w_mod[i] + b_mod[i], N_MOD, axis=-1)
        sh1, sc1, g1, sh2, sc2, g2 = (t[:, None, :] for t in (sh1, sc1, g1, sh2, sc2, g2))
        csh1, csc1, cg1, csh2, csc2, cg2 = jnp.split(silu_cc @ w_mod[i] + b_mod[i], N_MOD, axis=-1)

        h = modulate(rms_norm(x, norm1_g[i]), sh1, sc1)
        hc = modulate(rms_norm(ctx, norm1_g[i]), csh1, csc1)
        z = h @ w_in[i]
        a = glu(z[..., :2 * D_CONV])
        a_cols = a.reshape(B_, rows, GRID_W, D_CONV).transpose(0, 2, 1, 3)
        a_cols = a_cols.reshape(B_ * GRID_W, rows, D_CONV)
        y_conv = conv_module(a_cols, conv_w[i], conv_b[i], conv_ln_g[i], conv_ln_b[i])
        y_conv = y_conv.reshape(B_, GRID_W, rows, D_CONV).transpose(0, 2, 1, 3)
        y_conv = y_conv.reshape(B_, L, D_CONV)
        u_ctx = hc @ w_in[i][:, 2 * D_CONV:]
        y_ssm, y_ssm_ctx = s5_block(u_ctx, z[..., 2 * D_CONV:], ssm_a_re[i], ssm_a_im[i],
                                    ssm_log_dt[i], ssm_b_re[i], ssm_b_im[i], ssm_c_re[i],
                                    ssm_c_im[i], ssm_d[i], ssm_w_glu[i], ssm_b_glu[i], need_ctx)
        mix = jnp.concatenate([y_conv, y_ssm], axis=-1) @ w_out[i]
        x = x + g1 * mix

        h2 = modulate(rms_norm(x, norm2_g[i]), sh2, sc2)
        x = x + g2 * swiglu(h2, w_gate_up[i], w_down[i])

        if need_ctx:
            ac = glu(hc @ w_in[i][:, :2 * D_CONV])
            y_conv_c = conv_module(ac, conv_w[i], conv_b[i], conv_ln_g[i], conv_ln_b[i])
            ctx = ctx + cg1 * (jnp.concatenate([y_conv_c, y_ssm_ctx], axis=-1) @ w_out[i])
            hc2 = modulate(rms_norm(ctx, norm2_g[i]), csh2, csc2)
            ctx = ctx + cg2 * swiglu(hc2, w_gate_up[i], w_down[i])
    return rms_norm(x, final_norm_g)
```

```python
import functools

import jax
import jax.numpy as jnp
from jax import lax
from jax.experimental import pallas as pl
from jax.experimental.pallas import tpu as pltpu

F32 = jnp.float32
BF16 = jnp.bfloat16
HI = lax.Precision.HIGHEST

D_MODEL = 1024
SEQ = 16384
GRID_W = 64
CTX_LEN = 256
D_CONV = 512
D_SSM = 512
SSM_GROUP = 16
GROUP_SHIFT = 4
N_GROUPS = 32
STATE = 64
CONV_WIDTH = 31
D_FF = 2816
N_MOD = 6
EPS = 1e-6

CHUNK = 32
NC_LAT = SEQ // CHUNK
NC_CTX = CTX_LEN // CHUNK
NC = NC_LAT + NC_CTX
CW = CHUNK * SSM_GROUP
SW = 4 * STATE
TAB = 128

TM_IN = 512
TM_CONV = 1024
HALO = (CONV_WIDTH // 2) * GRID_W
TM_TAIL = 256
VMEM_LIMIT = 56 * 1024 * 1024


def _sigmoid(v):
    return 1.0 / (1.0 + jnp.exp(-v))


def _rms(v, g):
    ms = jnp.mean(v * v, axis=-1, keepdims=True)
    return v * lax.rsqrt(ms + EPS) * g


def _mod_kernel(c_ref, w_ref, b_ref, o_ref):
    c = c_ref[...]
    s = c * _sigmoid(c)
    o_ref[...] = jnp.dot(s, w_ref[...], precision=HI, preferred_element_type=F32) + b_ref[...]


def _mod(cvec, w_mod, b_mod):
    return pl.pallas_call(
        _mod_kernel,
        out_shape=jax.ShapeDtypeStruct((8, N_MOD * D_MODEL), F32),
        grid=(N_MOD,),
        in_specs=[
            pl.BlockSpec((8, D_MODEL), lambda j: (0, 0)),
            pl.BlockSpec((D_MODEL, D_MODEL), lambda j: (0, j)),
            pl.BlockSpec((1, D_MODEL), lambda j: (0, j)),
        ],
        out_specs=pl.BlockSpec((8, D_MODEL), lambda j: (0, j)),
    )(cvec, w_mod, b_mod)


def _inproj_kernel(x_ref, mod_ref, g_ref, w_ref, *out_refs, with_conv):
    h = _rms(x_ref[...], g_ref[...])
    h = h * (1.0 + mod_ref[1:2, :]) + mod_ref[0:1, :]
    z = jnp.dot(h.astype(BF16), w_ref[...], preferred_element_type=F32)
    if with_conv:
        a_ref, u_ref = out_refs
        a_ref[...] = z[:, :D_CONV] * _sigmoid(z[:, D_CONV:2 * D_CONV])
        u_ref[...] = z[:, 2 * D_CONV:]
    else:
        (u_ref,) = out_refs
        u_ref[...] = z


def _inproj(x2d, mods, norm_g, w, with_conv, tm):
    n = x2d.shape[0]
    width = w.shape[1]
    out_shape = [jax.ShapeDtypeStruct((n, D_SSM), F32)]
    out_specs = [pl.BlockSpec((tm, D_SSM), lambda i: (i, 0))]
    if with_conv:
        out_shape = [jax.ShapeDtypeStruct((n, D_CONV), F32)] + out_shape
        out_specs = [pl.BlockSpec((tm, D_CONV), lambda i: (i, 0))] + out_specs
    return pl.pallas_call(
        functools.partial(_inproj_kernel, with_conv=with_conv),
        out_shape=out_shape,
        grid=(n // tm,),
        in_specs=[
            pl.BlockSpec((tm, D_MODEL), lambda i: (i, 0)),
            pl.BlockSpec((N_MOD, D_MODEL), lambda i: (0, 0)),
            pl.BlockSpec((1, D_MODEL), lambda i: (0, 0)),
            pl.BlockSpec((D_MODEL, width), lambda i: (0, 0)),
        ],
        out_specs=out_specs,
        compiler_params=pltpu.CompilerParams(vmem_limit_bytes=VMEM_LIMIT),
    )(x2d, mods, norm_g, w)


def _ssm_w_kernel(arow_ref, acol_ref, ldt_ref, bt_ref, ct_ref,
                  wx_ref, toep_ref, wc_ref, dec_ref):
    tau_l = jnp.minimum(lax.broadcasted_iota(jnp.int32, (1, TAB), 1), CHUNK).astype(F32)
    tau_r = jnp.minimum(lax.broadcasted_iota(jnp.int32, (TAB, 1), 0), CHUNK).astype(F32)
    tab_row = lax.broadcasted_iota(jnp.int32, (TAB, 1), 0)
    tab_lane = lax.broadcasted_iota(jnp.int32, (1, TAB), 1)

    lane_y = lax.broadcasted_iota(jnp.int32, (1, CW), 1)
    t_of_lane = lane_y >> GROUP_SHIFT
    lane_g = lax.broadcasted_iota(jnp.int32, (1, 2 * CW), 1)
    j_of_lane = lane_g >> GROUP_SHIFT
    s_of_row = lax.broadcasted_iota(jnp.int32, (CW, 1), 0) >> GROUP_SHIFT
    co_sel_y = (lax.broadcasted_iota(jnp.int32, (SSM_GROUP, 1), 0)
                == (lane_y & (SSM_GROUP - 1))).astype(F32)
    co_sel_g = (lax.broadcasted_iota(jnp.int32, (SSM_GROUP, 1), 0)
                == (lane_g & (SSM_GROUP - 1))).astype(F32)

    def dotx(a, b):
        return jnp.dot(a, b, precision=HI, preferred_element_type=F32)

    g_pad = jnp.zeros((SSM_GROUP, 2 * CW), F32)
    for d in range(2):
        dt = jnp.exp(ldt_ref[d])
        ar_r = arow_ref[d, 0:1, :]
        ai_r = arow_ref[d, 1:2, :]
        ar_c = acol_ref[d, :, 0:1]
        ai_c = acol_ref[d, :, 1:2]

        mag_c = jnp.exp(ar_c * dt * tau_l)
        ph_c = ai_c * dt * tau_l
        pc_re, pc_im = mag_c * jnp.cos(ph_c), mag_c * jnp.sin(ph_c)
        mag_r = jnp.exp(tau_r * (ar_r * dt))
        ph_r = tau_r * (ai_r * dt)
        pr_re, pr_im = mag_r * jnp.cos(ph_r), mag_r * jnp.sin(ph_r)

        ab_re, ab_im = pr_re[1:2, :], pr_im[1:2, :]
        n_re, n_im = ab_re - 1.0, ab_im
        den = ar_r * ar_r + ai_r * ai_r
        q_re = (n_re * ar_r + n_im * ai_r) / den
        q_im = (n_im * ar_r - n_re * ai_r) / den
        b_re, b_im = bt_ref[d, 0], bt_ref[d, 1]
        bb_re = q_re * b_re - q_im * b_im
        bb_im = q_re * b_im + q_im * b_re

        c_re, c_im = ct_ref[d, 0], ct_ref[d, 1]

        e_s = (CHUNK - 1 - s_of_row) if d == 0 else s_of_row
        sel_s = (e_s == tab_lane).astype(F32)
        px_re, px_im = dotx(sel_s, pr_re), dotx(sel_s, pr_im)
        bt_re = jnp.broadcast_to(bb_re[None], (CHUNK, SSM_GROUP, STATE)).reshape(CW, STATE)
        bt_im = jnp.broadcast_to(bb_im[None], (CHUNK, SSM_GROUP, STATE)).reshape(CW, STATE)
        wx_ref[:, d * STATE:(d + 1) * STATE] = (px_re * bt_re - px_im * bt_im).astype(wx_ref.dtype)
        wx_ref[:, (2 + d) * STATE:(3 + d) * STATE] = (px_re * bt_im + px_im * bt_re).astype(wx_ref.dtype)

        e_t = (t_of_lane + 1) if d == 0 else (CHUNK - t_of_lane)
        sel_t = (tab_row == e_t).astype(F32)
        py_re, py_im = dotx(pc_re, sel_t), dotx(pc_im, sel_t)
        cy_re, cy_im = dotx(c_re, co_sel_y), dotx(c_im, co_sel_y)
        wc_ref[d * STATE:(d + 1) * STATE, :] = (cy_re * py_re - cy_im * py_im).astype(wc_ref.dtype)
        wc_ref[(2 + d) * STATE:(3 + d) * STATE, :] = (-(cy_re * py_im + cy_im * py_re)).astype(wc_ref.dtype)

        e_j = (j_of_lane - (CHUNK - 1)) if d == 0 else ((CHUNK - 1) - j_of_lane)
        e_j = jnp.where((e_j >= 0) & (e_j < CHUNK), e_j, -1)
        sel_j = (tab_row == e_j).astype(F32)
        pg_re, pg_im = dotx(pc_re, sel_j), dotx(pc_im, sel_j)
        cg_re, cg_im = dotx(c_re, co_sel_g), dotx(c_im, co_sel_g)
        m_re = cg_re * pg_re - cg_im * pg_im
        m_im = cg_re * pg_im + cg_im * pg_re
        g_pad = g_pad + dotx(bb_re, m_re) - dotx(bb_im, m_im)

        dec_ref[:, d * STATE:(d + 1) * STATE] = pr_re[CHUNK:CHUNK + 1, :]
        dec_ref[:, (2 + d) * STATE:(3 + d) * STATE] = pr_im[CHUNK:CHUNK + 1, :]

    for s in range(CHUNK):
        k = (CHUNK - 1 - s) * SSM_GROUP
        rolled = g_pad if k == 0 else pltpu.roll(g_pad, 2 * CW - k, axis=1)
        toep_ref[s * SSM_GROUP:(s + 1) * SSM_GROUP, :] = rolled[:, :CW].astype(toep_ref.dtype)


def _ssm_weights(arow, acol, ldt, bt, ct):
    def spec(*tail):
        zeros = (0,) * len(tail)
        return pl.BlockSpec((None,) + tail, lambda g: (g,) + zeros)
    return pl.pallas_call(
        _ssm_w_kernel,
        out_shape=[
            jax.ShapeDtypeStruct((N_GROUPS, CW, SW), BF16),
            jax.ShapeDtypeStruct((N_GROUPS, CW, CW), BF16),
            jax.ShapeDtypeStruct((N_GROUPS, SW, CW), BF16),
            jax.ShapeDtypeStruct((N_GROUPS, 1, SW), F32),
        ],
        grid=(N_GROUPS,),
        in_specs=[spec(2, 2, STATE), spec(2, STATE, 2), spec(2, 1, 1),
                  spec(2, 2, SSM_GROUP, STATE), spec(2, 2, STATE, SSM_GROUP)],
        out_specs=[spec(CW, SW), spec(CW, CW), spec(SW, CW), spec(1, SW)],
    )(arow, acol, ldt, bt, ct)


def _ssm_x_kernel(u_ref, wx_ref, x_ref):
    x_ref[...] = jnp.dot(u_ref[...], wx_ref[...], preferred_element_type=F32)


def _ssm_x(u_g, wx):
    return pl.pallas_call(
        _ssm_x_kernel,
        out_shape=jax.ShapeDtypeStruct((N_GROUPS, NC, SW), F32),
        grid=(N_GROUPS,),
        in_specs=[pl.BlockSpec((None, NC, CW), lambda g: (g, 0, 0)),
                  pl.BlockSpec((None, CW, SW), lambda g: (g, 0, 0))],
        out_specs=pl.BlockSpec((None, NC, SW), lambda g: (g, 0, 0)),
    )(u_g, wx)


def _ssm_scan_kernel(x_ref, dec_ref, s_ref):
    half = 2 * STATE
    d_re = dec_ref[:, 0:half]
    d_im = dec_ref[:, half:2 * half]
    is_fwd = lax.broadcasted_iota(jnp.int32, (N_GROUPS, half), 1) < STATE

    def body(k, carry):
        s_re, s_im = carry
        cf = jnp.where(k < NC_CTX, NC_LAT + k, k - NC_CTX)
        cb = NC - 1 - k
        s_ref[cf, :, 0:STATE] = s_re[:, 0:STATE]
        s_ref[cf, :, half:half + STATE] = s_im[:, 0:STATE]
        s_ref[cb, :, STATE:half] = s_re[:, STATE:half]
        s_ref[cb, :, half + STATE:2 * half] = s_im[:, STATE:half]
        xf = x_ref[cf]
        xb = x_ref[cb]
        x_re = jnp.where(is_fwd, xf[:, 0:half], xb[:, 0:half])
        x_im = jnp.where(is_fwd, xf[:, half:], xb[:, half:])
        n_re = d_re * s_re - d_im * s_im + x_re
        n_im = d_re * s_im + d_im * s_re + x_im
        return n_re, n_im

    zero = jnp.zeros((N_GROUPS, half), F32)
    lax.fori_loop(0, NC, body, (zero, zero))


def _ssm_scan(x_t, dec):
    vmem = pl.BlockSpec(memory_space=pltpu.VMEM)
    return pl.pallas_call(
        _ssm_scan_kernel,
        out_shape=jax.ShapeDtypeStruct((NC, N_GROUPS, SW), F32),
        in_specs=[vmem, vmem],
        out_specs=vmem,
        compiler_params=pltpu.CompilerParams(vmem_limit_bytes=VMEM_LIMIT),
    )(x_t, dec)


def _ssm_y_kernel(u_ref, s_ref, toep_ref, wc_ref, y_ref):
    y = jnp.dot(u_ref[...], toep_ref[...], preferred_element_type=F32)
    y = y + jnp.dot(s_ref[...].astype(BF16), wc_ref[...], preferred_element_type=F32)
    y_ref[...] = y


def _ssm_y(u_g, s_g, toep, wc):
    return pl.pallas_call(
        _ssm_y_kernel,
        out_shape=jax.ShapeDtypeStruct((N_GROUPS, NC_LAT, CW), F32),
        grid=(N_GROUPS,),
        in_specs=[pl.BlockSpec((None, NC_LAT, CW), lambda g: (g, 0, 0)),
                  pl.BlockSpec((None, NC_LAT, SW), lambda g: (g, 0, 0)),
                  pl.BlockSpec((None, CW, CW), lambda g: (g, 0, 0)),
                  pl.BlockSpec((None, SW, CW), lambda g: (g, 0, 0))],
        out_specs=pl.BlockSpec((None, NC_LAT, CW), lambda g: (g, 0, 0)),
    )(u_g, s_g, toep, wc)


def _conv_kernel(ap_ref, ac_ref, an_ref, w_ref, b_ref, lg_ref, lb_ref, o_ref, win_ref):
    i = pl.program_id(0)
    last = pl.num_programs(0) - 1

    @pl.when(i > 0)
    def _():
        win_ref[0:HALO, :] = ap_ref[TM_CONV - HALO:TM_CONV, :]

    @pl.when(i == 0)
    def _():
        win_ref[0:HALO, :] = jnp.zeros((HALO, D_CONV), F32)

    win_ref[HALO:HALO + TM_CONV, :] = ac_ref[...]

    @pl.when(i < last)
    def _():
        win_ref[HALO + TM_CONV:, :] = an_ref[0:HALO, :]

    @pl.when(i == last)
    def _():
        win_ref[HALO + TM_CONV:, :] = jnp.zeros((HALO, D_CONV), F32)

    def body(j, carry):
        r0 = pl.multiple_of(j * GRID_W, GRID_W)
        acc = jnp.zeros((GRID_W, D_CONV), F32)
        for k in range(CONV_WIDTH):
            acc = acc + win_ref[pl.ds(r0 + k * GRID_W, GRID_W), :] * w_ref[k:k + 1, :]
        y = acc + b_ref[...]
        mu = jnp.mean(y, axis=-1, keepdims=True)
        yc = y - mu
        var = jnp.mean(yc * yc, axis=-1, keepdims=True)
        yn = yc * lax.rsqrt(var + EPS) * lg_ref[...] + lb_ref[...]
        o_ref[pl.ds(r0, GRID_W), :] = yn * _sigmoid(yn)
        return carry

    lax.fori_loop(0, TM_CONV // GRID_W, body, 0)


def _conv(a, conv_w, conv_b, ln_g, ln_b):
    nb = SEQ // TM_CONV
    row = lambda i: (0, 0)
    return pl.pallas_call(
        _conv_kernel,
        out_shape=jax.ShapeDtypeStruct((SEQ, D_CONV), F32),
        grid=(nb,),
        in_specs=[
            pl.BlockSpec((TM_CONV, D_CONV), lambda i: (jnp.maximum(i - 1, 0), 0)),
            pl.BlockSpec((TM_CONV, D_CONV), lambda i: (i, 0)),
            pl.BlockSpec((TM_CONV, D_CONV), lambda i: (jnp.minimum(i + 1, nb - 1), 0)),
            pl.BlockSpec((CONV_WIDTH, D_CONV), row),
            pl.BlockSpec((1, D_CONV), row),
            pl.BlockSpec((1, D_CONV), row),
            pl.BlockSpec((1, D_CONV), row),
        ],
        out_specs=pl.BlockSpec((TM_CONV, D_CONV), lambda i: (i, 0)),
        scratch_shapes=[pltpu.VMEM((TM_CONV + 2 * HALO, D_CONV), F32)],
        compiler_params=pltpu.CompilerParams(vmem_limit_bytes=VMEM_LIMIT),
    )(a, a, a, conv_w, conv_b, ln_g, ln_b)


def _tail_kernel(x_ref, yc_ref, ys_ref, u_ref, mod_ref, dskip_ref, wglu_ref, bglu_ref,
                 wout_ref, n2g_ref, wgu_ref, wdn_ref, fg_ref, o_ref):
    g1 = mod_ref[2:3, :]
    sh2 = mod_ref[3:4, :]
    sc2 = mod_ref[4:5, :]
    g2 = mod_ref[5:6, :]

    y = ys_ref[...] + u_ref[...] * dskip_ref[...]
    gl = jax.nn.gelu(y, approximate=True)
    gate = jnp.dot(gl.astype(BF16), wglu_ref[...], preferred_element_type=F32) + bglu_ref[...]
    y_ssm = gl * _sigmoid(gate)

    mix = jnp.dot(yc_ref[...].astype(BF16), wout_ref[0:D_CONV, :], preferred_element_type=F32)
    mix = mix + jnp.dot(y_ssm.astype(BF16), wout_ref[D_CONV:, :], preferred_element_type=F32)
    x1 = x_ref[...] + g1 * mix

    h2 = _rms(x1, n2g_ref[...]) * (1.0 + sc2) + sh2
    gu = jnp.dot(h2.astype(BF16), wgu_ref[...], preferred_element_type=F32)
    gt = gu[:, :D_FF]
    act = gt * _sigmoid(gt) * gu[:, D_FF:]
    ffn = jnp.dot(act.astype(BF16), wdn_ref[...], preferred_element_type=F32)
    x2 = x1 + g2 * ffn
    o_ref[...] = _rms(x2, fg_ref[...])


def _tail(x2d, yc, ys, u, mods, dskip, wglu, bglu, wout, n2g, wgu, wdn, fg):
    tm = TM_TAIL
    const = lambda i: (0, 0)
    tile = lambda i: (i, 0)

    def resident(shape):
        return pl.BlockSpec(shape, const, pipeline_mode=pl.Buffered(1))

    return pl.pallas_call(
        _tail_kernel,
        out_shape=jax.ShapeDtypeStruct((SEQ, D_MODEL), F32),
        grid=(SEQ // tm,),
        in_specs=[
            pl.BlockSpec((tm, D_MODEL), tile),
            pl.BlockSpec((tm, D_CONV), tile),
            pl.BlockSpec((tm, D_SSM), tile),
            pl.BlockSpec((tm, D_SSM), tile),
            resident((N_MOD, D_MODEL)),
            resident((1, D_SSM)),
            resident((D_SSM, D_SSM)),
            resident((1, D_SSM)),
            resident((D_MODEL, D_MODEL)),
            resident((1, D_MODEL)),
            resident((D_MODEL, 2 * D_FF)),
            resident((D_FF, D_MODEL)),
            resident((1, D_MODEL)),
        ],
        out_specs=pl.BlockSpec((tm, D_MODEL), tile),
        compiler_params=pltpu.CompilerParams(vmem_limit_bytes=VMEM_LIMIT),
    )(x2d, yc, ys, u, mods, dskip, wglu, bglu, wout, n2g, wgu, wdn, fg)


def _to_group_chunks(u):
    n = u.shape[0] // CHUNK
    u4 = u.astype(BF16).reshape(n, CHUNK, N_GROUPS, SSM_GROUP)
    return u4.transpose(2, 0, 1, 3).reshape(N_GROUPS, n, CW)


def kernel(x, c, ctx, c_ctx, w_mod, b_mod, norm1_g, w_in, conv_w, conv_b, conv_ln_g, conv_ln_b,
           ssm_a_re, ssm_a_im, ssm_log_dt, ssm_b_re, ssm_b_im, ssm_c_re, ssm_c_im, ssm_d,
           ssm_w_glu, ssm_b_glu, w_out, norm2_g, w_gate_up, w_down, final_norm_g):
    assert x.shape == (1, SEQ, D_MODEL) and ctx.shape == (1, CTX_LEN, D_MODEL)
    assert w_mod.shape[0] == 1, "single trunk layer"
    x2d = x[0]
    ctx2d = ctx[0]

    cvec = jnp.zeros((8, D_MODEL), F32).at[0].set(c[0]).at[1].set(c_ctx)
    mods_all = _mod(cvec, w_mod[0], b_mod[0][None, :])
    mods = mods_all[0].reshape(N_MOD, D_MODEL)
    mods_ctx = mods_all[1].reshape(N_MOD, D_MODEL)

    w_in_b = w_in[0].astype(BF16)
    n1g = norm1_g[0][None, :]
    a, u = _inproj(x2d, mods, n1g, w_in_b, True, TM_IN)
    (u_ctx,) = _inproj(ctx2d, mods_ctx, n1g, w_in_b[:, 2 * D_CONV:], False, CTX_LEN)

    a_re, a_im = ssm_a_re[0], ssm_a_im[0]
    arow = jnp.stack([a_re, a_im], axis=2).transpose(1, 0, 2, 3)
    acol = jnp.stack([a_re, a_im], axis=3).transpose(1, 0, 2, 3)
    ldt = ssm_log_dt[0].transpose(1, 0)[:, :, None, None]
    bt = jnp.stack([ssm_b_re[0], ssm_b_im[0]], axis=2).transpose(1, 0, 2, 4, 3)
    ct = jnp.stack([ssm_c_re[0], ssm_c_im[0]], axis=2).transpose(1, 0, 2, 4, 3)
    wx, toep, wc, dec = _ssm_weights(arow, acol, ldt, bt, ct)

    u_g = jnp.concatenate([_to_group_chunks(u), _to_group_chunks(u_ctx)], axis=1)
    x_st = _ssm_x(u_g, wx)
    s_st = _ssm_scan(x_st.transpose(1, 0, 2), dec.reshape(N_GROUPS, SW))
    y_g = _ssm_y(u_g, s_st.transpose(1, 0, 2), toep, wc)
    y_raw = y_g.reshape(N_GROUPS, NC_LAT, CHUNK, SSM_GROUP).transpose(1, 2, 0, 3).reshape(SEQ, D_SSM)

    y_conv = _conv(a, conv_w[0], conv_b[0][None, :], conv_ln_g[0][None, :], conv_ln_b[0][None, :])

    out = _tail(x2d, y_conv, y_raw, u, mods, ssm_d[0][None, :], ssm_w_glu[0].astype(BF16),
                ssm_b_glu[0][None, :], w_out[0].astype(BF16), norm2_g[0][None, :],
                w_gate_up[0].astype(BF16), w_down[0].astype(BF16), final_norm_g[None, :])
    return out[None]
```

```python
import functools

import jax
import jax.numpy as jnp
from jax import lax
from jax.experimental import pallas as pl
from jax.experimental.pallas import tpu as pltpu

F32 = jnp.float32
BF16 = jnp.bfloat16
HI = lax.Precision.HIGHEST

D_MODEL = 1024
SEQ = 16384
GRID_W = 64
CTX_LEN = 256
D_CONV = 512
D_SSM = 512
SSM_GROUP = 16
GROUP_SHIFT = 4
N_GROUPS = 32
STATE = 64
CONV_WIDTH = 31
D_FF = 2816
N_MOD = 6
EPS = 1e-6

CHUNK = 32
NC_LAT = SEQ // CHUNK
NC_CTX = CTX_LEN // CHUNK
NC = NC_LAT + NC_CTX
CW = CHUNK * SSM_GROUP
SW = 4 * STATE
TAB = 128
LANES = 128
SUBLANES = 8

ROW_DIL = GRID_W // CHUNK
CONV_PAD = 32
CONV_ROWS = NC_LAT + 2 * CONV_PAD
CONV_PHASES = SUBLANES // ROW_DIL
CONV_BLOCK = 64
TAIL_HALVES = 2
VMEM_LIMIT = 56 * 1024 * 1024


def _sigmoid(v):
    return 1.0 / (1.0 + jnp.exp(-v))


def _rms(v, g):
    ms = jnp.mean(v * v, axis=-1, keepdims=True)
    return v * lax.rsqrt(ms + EPS) * g


def _dot(a, b):
    return jnp.dot(a, b, preferred_element_type=F32)


def _dot_nt(a, b):
    return lax.dot_general(a, b, (((1,), (1,)), ((), ())), preferred_element_type=F32)


def _dot_tn(a, b):
    return lax.dot_general(a, b, (((0,), (0,)), ((), ())), preferred_element_type=F32)


def _row_gather(x_hbm, buf, sem, step, slot):
    return pltpu.make_async_copy(x_hbm.at[:, step, :], buf.at[slot], sem.at[slot])


def _row_scatter(buf, o_hbm, sem, step, slot):
    return pltpu.make_async_copy(buf.at[slot], o_hbm.at[:, step, :], sem.at[slot])


def _mod_kernel(c_ref, w_ref, b_ref, o_ref):
    c = c_ref[...]
    s = c * _sigmoid(c)
    o_ref[...] = jnp.dot(s, w_ref[...], precision=HI, preferred_element_type=F32) + b_ref[...]


def _mod(cvec, w_mod, b_mod):
    return pl.pallas_call(
        _mod_kernel,
        out_shape=jax.ShapeDtypeStruct((8, N_MOD * D_MODEL), F32),
        grid=(N_MOD,),
        in_specs=[
            pl.BlockSpec((8, D_MODEL), lambda j: (0, 0)),
            pl.BlockSpec((D_MODEL, D_MODEL), lambda j: (0, j)),
            pl.BlockSpec((1, D_MODEL), lambda j: (0, j)),
        ],
        out_specs=pl.BlockSpec((8, D_MODEL), lambda j: (0, j)),
        name="mod",
    )(cvec, w_mod, b_mod)


def _inproj_kernel(x_hbm, mod_ref, g_ref, wc_ref, wst_ref, cw_ref, cb_ref, lg_ref, lb_ref,
                   yc_ref, ut_ref, xbuf, sem, apad, ash, ycv):
    s = pl.program_id(0)
    n = pl.num_programs(0)
    slot = s % 2

    @pl.when(s == 0)
    def _():
        _row_gather(x_hbm, xbuf, sem, 0, 0).start()
        zeros = jnp.zeros((CONV_PAD, D_CONV), F32)
        apad[0:CONV_PAD, :] = zeros
        apad[CONV_PAD + NC_LAT:CONV_ROWS, :] = zeros

    @pl.when(s + 1 < n)
    def _():
        _row_gather(x_hbm, xbuf, sem, s + 1, 1 - slot).start()

    _row_gather(x_hbm, xbuf, sem, s, slot).wait()

    h = _rms(xbuf[slot], g_ref[...])
    h = (h * (1.0 + mod_ref[1:2, :]) + mod_ref[0:1, :]).astype(BF16)
    z = _dot(h, wc_ref[...])
    apad[CONV_PAD:CONV_PAD + NC_LAT, :] = z[:, :D_CONV] * _sigmoid(z[:, D_CONV:])
    zt = _dot_nt(wst_ref[...], h)
    ut_ref[...] = zt.reshape(N_GROUPS, SSM_GROUP, NC_LAT).astype(ut_ref.dtype)

    span = CONV_ROWS - SUBLANES
    for m in range(1, CONV_PHASES):
        ash[m - 1, 0:span, :] = apad[pl.ds(ROW_DIL * m, span), :]

    def conv_block(rb, carry):
        r0 = pl.multiple_of(rb * CONV_BLOCK, CONV_BLOCK)
        for lt in range(D_CONV // LANES):
            lanes = slice(lt * LANES, (lt + 1) * LANES)
            acc = jnp.zeros((CONV_BLOCK, LANES), F32)
            for k in range(CONV_WIDTH):
                off = CONV_PAD + ROW_DIL * (k - CONV_WIDTH // 2)
                m, q = (off // ROW_DIL) % CONV_PHASES, off // SUBLANES
                src = apad if m == 0 else ash.at[m - 1]
                acc = acc + src[pl.ds(r0 + SUBLANES * q, CONV_BLOCK), lanes] * cw_ref[k:k + 1, lanes]
            ycv[pl.ds(r0, CONV_BLOCK), lanes] = acc
        y = ycv[pl.ds(r0, CONV_BLOCK), :] + cb_ref[...]
        mu = jnp.mean(y, axis=-1, keepdims=True)
        yc = y - mu
        var = jnp.mean(yc * yc, axis=-1, keepdims=True)
        yn = yc * lax.rsqrt(var + EPS) * lg_ref[...] + lb_ref[...]
        yc_ref[pl.ds(r0, CONV_BLOCK), :] = (yn * _sigmoid(yn)).astype(yc_ref.dtype)
        return carry

    lax.fori_loop(0, NC_LAT // CONV_BLOCK, conv_block, 0)


def _inproj(x3, mods, norm_g, w_conv, w_ssm_t, conv_w, conv_b, ln_g, ln_b):
    const = lambda s: (0, 0)
    return pl.pallas_call(
        _inproj_kernel,
        out_shape=[jax.ShapeDtypeStruct((CHUNK, NC_LAT, D_CONV), F32),
                   jax.ShapeDtypeStruct((N_GROUPS, CW, NC_LAT), BF16)],
        grid=(CHUNK,),
        in_specs=[
            pl.BlockSpec(memory_space=pl.ANY),
            pl.BlockSpec((N_MOD, D_MODEL), const),
            pl.BlockSpec((1, D_MODEL), const),
            pl.BlockSpec((D_MODEL, 2 * D_CONV), const),
            pl.BlockSpec((D_SSM, D_MODEL), const),
            pl.BlockSpec((CONV_WIDTH, D_CONV), const),
            pl.BlockSpec((1, D_CONV), const),
            pl.BlockSpec((1, D_CONV), const),
            pl.BlockSpec((1, D_CONV), const),
        ],
        out_specs=[pl.BlockSpec((None, NC_LAT, D_CONV), lambda s: (s, 0, 0)),
                   pl.BlockSpec((N_GROUPS, SSM_GROUP, NC_LAT), lambda s: (0, s, 0))],
        scratch_shapes=[
            pltpu.VMEM((2, NC_LAT, D_MODEL), F32),
            pltpu.SemaphoreType.DMA((2,)),
            pltpu.VMEM((CONV_ROWS, D_CONV), F32),
            pltpu.VMEM((CONV_PHASES - 1, CONV_ROWS, D_CONV), F32),
            pltpu.VMEM((NC_LAT, D_CONV), F32),
        ],
        compiler_params=pltpu.CompilerParams(dimension_semantics=("arbitrary",),
                                             vmem_limit_bytes=VMEM_LIMIT),
        name="inproj",
    )(x3, mods, norm_g, w_conv, w_ssm_t, conv_w, conv_b, ln_g, ln_b)


def _ctxproj_kernel(x_ref, mod_ref, g_ref, w_ref, u_ref):
    h = _rms(x_ref[...], g_ref[...])
    h = (h * (1.0 + mod_ref[1:2, :]) + mod_ref[0:1, :]).astype(BF16)
    u_ref[...] = _dot(h, w_ref[...])


def _ctxproj(ctx2d, mods, norm_g, w_ssm):
    return pl.pallas_call(
        _ctxproj_kernel,
        out_shape=jax.ShapeDtypeStruct((CTX_LEN, D_SSM), F32),
        name="ctxproj",
    )(ctx2d, mods, norm_g, w_ssm)


def _split2(v):
    hi = v.astype(BF16)
    lo = (v - hi.astype(F32)).astype(BF16)
    return hi, lo


def _expand_right(table, sel):
    hi, lo = _split2(table)
    return _dot(hi, sel) + _dot(lo, sel)


def _expand_left(sel, table):
    hi, lo = _split2(table)
    return _dot(sel, hi) + _dot(sel, lo)


def _ssm_w_kernel(arow_ref, acol_ref, ldt_ref, b_ref, c_ref, d_ref,
                  wxt_ref, toept_ref, wct_ref, dec_ref):
    i32 = jnp.int32
    tau_l = jnp.minimum(lax.broadcasted_iota(i32, (1, TAB), 1), CHUNK).astype(F32)
    tau_r = jnp.minimum(lax.broadcasted_iota(i32, (TAB, 1), 0), CHUNK).astype(F32)
    tab_row = lax.broadcasted_iota(i32, (TAB, 1), 0)
    tab_lane = lax.broadcasted_iota(i32, (1, TAB), 1)

    lane_x = lax.broadcasted_iota(i32, (1, CW), 1)
    s_of_lane = lane_x >> GROUP_SHIFT
    lane_g = lax.broadcasted_iota(i32, (1, 2 * CW), 1)
    j_of_lane = lane_g >> GROUP_SHIFT
    t_of_row = lax.broadcasted_iota(i32, (CW, 1), 0) >> GROUP_SHIFT
    ch_row = lax.broadcasted_iota(i32, (SSM_GROUP, 1), 0)
    ci_sel = (ch_row == (lane_x & (SSM_GROUP - 1))).astype(BF16)

    g_lhs, g_rhs = [], []
    for d in range(2):
        dt = jnp.exp(ldt_ref[d])
        ar_r, ai_r = arow_ref[d, 0:1, :], arow_ref[d, 1:2, :]
        ar_c, ai_c = acol_ref[d, :, 0:1], acol_ref[d, :, 1:2]

        mag_c = jnp.exp(ar_c * dt * tau_l)
        ph_c = ai_c * dt * tau_l
        pc_re, pc_im = mag_c * jnp.cos(ph_c), mag_c * jnp.sin(ph_c)
        mag_r = jnp.exp(tau_r * (ar_r * dt))
        ph_r = tau_r * (ai_r * dt)
        pr_re, pr_im = mag_r * jnp.cos(ph_r), mag_r * jnp.sin(ph_r)

        n_re, n_im = pc_re[:, 1:2] - 1.0, pc_im[:, 1:2]
        den = ar_c * ar_c + ai_c * ai_c
        q_re = (n_re * ar_c + n_im * ai_c) / den
        q_im = (n_im * ar_c - n_re * ai_c) / den
        b_re, b_im = b_ref[d, 0], b_ref[d, 1]
        bb_re = q_re * b_re - q_im * b_im
        bb_im = q_re * b_im + q_im * b_re
        bt_re, bt_im = _expand_right(bb_re, ci_sel), _expand_right(bb_im, ci_sel)

        e_s = (CHUNK - 1 - s_of_lane) if d == 0 else s_of_lane
        sel_s = (tab_row == e_s).astype(BF16)
        px_re, px_im = _expand_right(pc_re, sel_s), _expand_right(pc_im, sel_s)
        wxt_ref[d * STATE:(d + 1) * STATE, :] = (px_re * bt_re - px_im * bt_im).astype(wxt_ref.dtype)
        wxt_ref[(2 + d) * STATE:(3 + d) * STATE, :] = (px_re * bt_im + px_im * bt_re).astype(wxt_ref.dtype)

        e_t = (t_of_row + 1) if d == 0 else (CHUNK - t_of_row)
        sel_t = (e_t == tab_lane).astype(BF16)
        py_re, py_im = _expand_left(sel_t, pr_re), _expand_left(sel_t, pr_im)
        c_re, c_im = c_ref[d, 0], c_ref[d, 1]
        ct_re = jnp.broadcast_to(c_re[None], (CHUNK, SSM_GROUP, STATE)).reshape(CW, STATE)
        ct_im = jnp.broadcast_to(c_im[None], (CHUNK, SSM_GROUP, STATE)).reshape(CW, STATE)
        wct_ref[:, d * STATE:(d + 1) * STATE] = (ct_re * py_re - ct_im * py_im).astype(wct_ref.dtype)
        wct_ref[:, (2 + d) * STATE:(3 + d) * STATE] = (-(ct_re * py_im + ct_im * py_re)).astype(wct_ref.dtype)

        e_j = ((CHUNK - 1) - j_of_lane) if d == 0 else (j_of_lane - (CHUNK - 1))
        e_j = jnp.where((e_j >= 0) & (e_j < CHUNK), e_j, -1)
        sel_j = (tab_row == e_j).astype(BF16)
        pg_re, pg_im = _expand_right(pc_re, sel_j), _expand_right(pc_im, sel_j)
        bg_re = jnp.concatenate([bt_re, bt_re], axis=1)
        bg_im = jnp.concatenate([bt_im, bt_im], axis=1)
        m_re = pg_re * bg_re - pg_im * bg_im
        m_im = pg_re * bg_im + pg_im * bg_re
        for cv, mv in ((c_re, m_re), (-c_im, m_im)):
            c_hi, c_lo = _split2(cv)
            m_hi, m_lo = _split2(mv)
            g_lhs += [c_hi, c_hi, c_lo]
            g_rhs += [m_hi, m_lo, m_hi]

        dec_ref[:, d * STATE:(d + 1) * STATE] = pr_re[CHUNK:CHUNK + 1, :]
        dec_ref[:, (2 + d) * STATE:(3 + d) * STATE] = pr_im[CHUNK:CHUNK + 1, :]

    g_pad = sum(_dot(lhs, rhs) for lhs, rhs in zip(g_lhs, g_rhs))
    on_diag = lane_g == (CHUNK - 1) * SSM_GROUP + ch_row
    g_pad = g_pad + jnp.where(on_diag, d_ref[...], 0.0)
    for t in range(CHUNK):
        k = (CHUNK - 1 - t) * SSM_GROUP
        rolled = g_pad if k == 0 else pltpu.roll(g_pad, 2 * CW - k, axis=1)
        toept_ref[t * SSM_GROUP:(t + 1) * SSM_GROUP, :] = rolled[:, :CW].astype(toept_ref.dtype)


def _ssm_weights(arow, acol, ldt, b, c, dcol):
    def spec(*tail):
        zeros = (0,) * len(tail)
        return pl.BlockSpec((None,) + tail, lambda g: (g,) + zeros)
    return pl.pallas_call(
        _ssm_w_kernel,
        out_shape=[
            jax.ShapeDtypeStruct((N_GROUPS, SW, CW), BF16),
            jax.ShapeDtypeStruct((N_GROUPS, CW, CW), BF16),
            jax.ShapeDtypeStruct((N_GROUPS, CW, SW), BF16),
            jax.ShapeDtypeStruct((N_GROUPS, 1, SW), F32),
        ],
        grid=(N_GROUPS,),
        in_specs=[spec(2, 2, STATE), spec(2, STATE, 2), spec(2, 1, 1),
                  spec(2, 2, STATE, SSM_GROUP), spec(2, 2, SSM_GROUP, STATE), spec(SSM_GROUP, 1)],
        out_specs=[spec(SW, CW), spec(CW, CW), spec(CW, SW), spec(1, SW)],
        name="ssm_w",
    )(arow, acol, ldt, b, c, dcol)


def _ssm_x_kernel(ut_ref, utc_ref, wxt_ref, x_ref):
    x_ref[0:NC_LAT, :] = _dot(wxt_ref[...], ut_ref[...]).T
    x_ref[NC_LAT:NC, :] = _dot(wxt_ref[...], utc_ref[...]).T[0:NC_CTX, :]


def _ssm_x(ut, ut_ctx, wxt):
    return pl.pallas_call(
        _ssm_x_kernel,
        out_shape=jax.ShapeDtypeStruct((N_GROUPS, NC, SW), F32),
        grid=(N_GROUPS,),
        in_specs=[pl.BlockSpec((None, CW, NC_LAT), lambda g: (g, 0, 0)),
                  pl.BlockSpec((None, CW, LANES), lambda g: (g, 0, 0)),
                  pl.BlockSpec((None, SW, CW), lambda g: (g, 0, 0))],
        out_specs=pl.BlockSpec((None, NC, SW), lambda g: (g, 0, 0)),
        name="ssm_x",
    )(ut, ut_ctx, wxt)


def _ssm_scan_kernel(x_ref, dec_ref, s_ref):
    half = 2 * STATE
    d_re = dec_ref[:, 0:half]
    d_im = dec_ref[:, half:2 * half]
    is_fwd = lax.broadcasted_iota(jnp.int32, (N_GROUPS, half), 1) < STATE

    def body(k, carry):
        s_re, s_im = carry
        cf = jnp.where(k < NC_CTX, NC_LAT + k, k - NC_CTX)
        cb = NC - 1 - k
        s_ref[cf, :, 0:STATE] = s_re[:, 0:STATE]
        s_ref[cf, :, half:half + STATE] = s_im[:, 0:STATE]
        s_ref[cb, :, STATE:half] = s_re[:, STATE:half]
        s_ref[cb, :, half + STATE:2 * half] = s_im[:, STATE:half]
        xf = x_ref[cf]
        xb = x_ref[cb]
        x_re = jnp.where(is_fwd, xf[:, 0:half], xb[:, 0:half])
        x_im = jnp.where(is_fwd, xf[:, half:], xb[:, half:])
        n_re = d_re * s_re - d_im * s_im + x_re
        n_im = d_re * s_im + d_im * s_re + x_im
        return n_re, n_im

    zero = jnp.zeros((N_GROUPS, half), F32)
    lax.fori_loop(0, NC, body, (zero, zero))


def _ssm_scan(x_t, dec):
    vmem = pl.BlockSpec(memory_space=pltpu.VMEM)
    return pl.pallas_call(
        _ssm_scan_kernel,
        out_shape=jax.ShapeDtypeStruct((NC, N_GROUPS, SW), F32),
        in_specs=[vmem, vmem],
        out_specs=vmem,
        compiler_params=pltpu.CompilerParams(vmem_limit_bytes=VMEM_LIMIT),
        name="ssm_scan",
    )(x_t, dec)


def _ssm_y_kernel(ut_ref, s_ref, toept_ref, wct_ref, yt_ref):
    y = _dot(toept_ref[...], ut_ref[...])
    y = y + _dot_nt(wct_ref[...], s_ref[...].astype(BF16))
    yt_ref[...] = y.astype(yt_ref.dtype)


def _ssm_y(ut, s_g, toept, wct):
    return pl.pallas_call(
        _ssm_y_kernel,
        out_shape=jax.ShapeDtypeStruct((N_GROUPS, CW, NC_LAT), F32),
        grid=(N_GROUPS,),
        in_specs=[pl.BlockSpec((None, CW, NC_LAT), lambda g: (g, 0, 0)),
                  pl.BlockSpec((None, NC_LAT, SW), lambda g: (g, 0, 0)),
                  pl.BlockSpec((None, CW, CW), lambda g: (g, 0, 0)),
                  pl.BlockSpec((None, CW, SW), lambda g: (g, 0, 0))],
        out_specs=pl.BlockSpec((None, CW, NC_LAT), lambda g: (g, 0, 0)),
        name="ssm_y",
    )(ut, s_g, toept, wct)


def _tail_kernel(x_hbm, yc_ref, yt_ref, mod_ref, wglut_ref, bglu_ref, wout_ref, n2g_ref,
                 wgu_ref, wdn_ref, fg_ref, o_hbm, xbuf, obuf, sem, osem):
    t = pl.program_id(0)
    n = pl.num_programs(0)
    slot = t % 2

    @pl.when(t == 0)
    def _():
        _row_gather(x_hbm, xbuf, sem, 0, 0).start()

    @pl.when(t + 1 < n)
    def _():
        _row_gather(x_hbm, xbuf, sem, t + 1, 1 - slot).start()

    g1 = mod_ref[2:3, :]
    sh2 = mod_ref[3:4, :]
    sc2 = mod_ref[4:5, :]
    g2 = mod_ref[5:6, :]

    y_t = yt_ref[...].reshape(D_SSM, NC_LAT).astype(F32)
    gl = jax.nn.gelu(y_t, approximate=True)
    gate = _dot(wglut_ref[...], gl.astype(BF16)) + bglu_ref[...]
    yss_t = (gl * _sigmoid(gate)).astype(BF16)

    mix = _dot(yc_ref[...].astype(BF16), wout_ref[0:D_CONV, :]) + _dot_tn(yss_t, wout_ref[D_CONV:, :])

    _row_gather(x_hbm, xbuf, sem, t, slot).wait()
    x1 = xbuf[slot] + g1 * mix
    h2 = (_rms(x1, n2g_ref[...]) * (1.0 + sc2) + sh2).astype(BF16)

    @pl.when(t >= 2)
    def _():
        _row_scatter(obuf, o_hbm, osem, t - 2, slot).wait()

    rows = NC_LAT // TAIL_HALVES
    for hf in range(TAIL_HALVES):
        r = slice(hf * rows, (hf + 1) * rows)
        gu = _dot(h2[r], wgu_ref[...])
        gt = gu[:, :D_FF]
        act = (gt * _sigmoid(gt) * gu[:, D_FF:]).astype(BF16)
        x2 = x1[r] + g2 * _dot(act, wdn_ref[...])
        obuf[slot, r, :] = _rms(x2, fg_ref[...])

    _row_scatter(obuf, o_hbm, osem, t, slot).start()

    @pl.when(t == n - 1)
    def _():
        _row_scatter(obuf, o_hbm, osem, t - 1, 1 - slot).wait()
        _row_scatter(obuf, o_hbm, osem, t, slot).wait()


def _tail(x3, yc, yt, mods, wglut, bglu, wout, n2g, wgu, wdn, fg):
    const = lambda t: (0, 0)

    def resident(shape):
        return pl.BlockSpec(shape, const, pipeline_mode=pl.Buffered(1))

    return pl.pallas_call(
        _tail_kernel,
        out_shape=jax.ShapeDtypeStruct((NC_LAT, CHUNK, D_MODEL), F32),
        grid=(CHUNK,),
        in_specs=[
            pl.BlockSpec(memory_space=pl.ANY),
            pl.BlockSpec((None, NC_LAT, D_CONV), lambda t: (t, 0, 0)),
            pl.BlockSpec((N_GROUPS, SSM_GROUP, NC_LAT), lambda t: (0, t, 0)),
            resident((N_MOD, D_MODEL)),
            resident((D_SSM, D_SSM)),
            resident((D_SSM, 1)),
            resident((D_MODEL, D_MODEL)),
            resident((1, D_MODEL)),
            resident((D_MODEL, 2 * D_FF)),
            resident((D_FF, D_MODEL)),
            resident((1, D_MODEL)),
        ],
        out_specs=pl.BlockSpec(memory_space=pl.ANY),
        scratch_shapes=[
            pltpu.VMEM((2, NC_LAT, D_MODEL), F32),
            pltpu.VMEM((2, NC_LAT, D_MODEL), F32),
            pltpu.SemaphoreType.DMA((2,)),
            pltpu.SemaphoreType.DMA((2,)),
        ],
        compiler_params=pltpu.CompilerParams(dimension_semantics=("arbitrary",),
                                             vmem_limit_bytes=VMEM_LIMIT),
        name="tail",
    )(x3, yc, yt, mods, wglut, bglu, wout, n2g, wgu, wdn, fg)


def kernel(x, c, ctx, c_ctx, w_mod, b_mod, norm1_g, w_in, conv_w, conv_b, conv_ln_g, conv_ln_b,
           ssm_a_re, ssm_a_im, ssm_log_dt, ssm_b_re, ssm_b_im, ssm_c_re, ssm_c_im, ssm_d,
           ssm_w_glu, ssm_b_glu, w_out, norm2_g, w_gate_up, w_down, final_norm_g):
    assert x.shape == (1, SEQ, D_MODEL) and ctx.shape == (1, CTX_LEN, D_MODEL)
    assert w_mod.shape[0] == 1, "single trunk layer"
    x3 = x.reshape(NC_LAT, CHUNK, D_MODEL)

    cvec = jnp.zeros((8, D_MODEL), F32).at[0].set(c[0]).at[1].set(c_ctx)
    mods_all = _mod(cvec, w_mod[0], b_mod[0][None, :])
    mods = mods_all[0].reshape(N_MOD, D_MODEL)
    mods_ctx = mods_all[1].reshape(N_MOD, D_MODEL)

    w_in_b = w_in[0].astype(BF16)
    w_ssm = w_in_b[:, 2 * D_CONV:]
    n1g = norm1_g[0][None, :]
    y_conv, ut = _inproj(x3, mods, n1g, w_in_b[:, :2 * D_CONV], w_ssm.T, conv_w[0],
                         conv_b[0][None, :], conv_ln_g[0][None, :], conv_ln_b[0][None, :])
    u_ctx = _ctxproj(ctx[0], mods_ctx, n1g, w_ssm)
    ut_ctx = u_ctx.reshape(NC_CTX, CHUNK, N_GROUPS, SSM_GROUP).transpose(2, 1, 3, 0)
    ut_ctx = jnp.pad(ut_ctx.reshape(N_GROUPS, CW, NC_CTX).astype(BF16),
                     ((0, 0), (0, 0), (0, LANES - NC_CTX)))

    a_re, a_im = ssm_a_re[0], ssm_a_im[0]
    arow = jnp.stack([a_re, a_im], axis=2).transpose(1, 0, 2, 3)
    acol = jnp.stack([a_re, a_im], axis=3).transpose(1, 0, 2, 3)
    ldt = ssm_log_dt[0].transpose(1, 0)[:, :, None, None]
    b_p = jnp.stack([ssm_b_re[0], ssm_b_im[0]], axis=2).transpose(1, 0, 2, 3, 4)
    c_p = jnp.stack([ssm_c_re[0], ssm_c_im[0]], axis=2).transpose(1, 0, 2, 3, 4)
    dcol = ssm_d[0].reshape(N_GROUPS, SSM_GROUP, 1)
    wxt, toept, wct, dec = _ssm_weights(arow, acol, ldt, b_p, c_p, dcol)

    x_st = _ssm_x(ut, ut_ctx, wxt)
    s_st = _ssm_scan(x_st.transpose(1, 0, 2), dec.reshape(N_GROUPS, SW))
    yt = _ssm_y(ut, s_st.transpose(1, 0, 2), toept, wct)

    out3 = _tail(x3, y_conv, yt, mods, ssm_w_glu[0].T.astype(BF16), ssm_b_glu[0][:, None],
                 w_out[0].astype(BF16), norm2_g[0][None, :], w_gate_up[0].astype(BF16),
                 w_down[0].astype(BF16), final_norm_g[None, :])
    return out3.reshape(1, SEQ, D_MODEL)
```

```python
import functools

import jax
import jax.numpy as jnp
from jax import lax
from jax.experimental import pallas as pl
from jax.experimental.pallas import tpu as pltpu

F32 = jnp.float32
BF16 = jnp.bfloat16
HI = lax.Precision.HIGHEST

D_MODEL = 1024
SEQ = 16384
GRID_W = 64
CTX_LEN = 256
D_CONV = 512
D_SSM = 512
SSM_GROUP = 16
GROUP_SHIFT = 4
N_GROUPS = 32
STATE = 64
CONV_WIDTH = 31
D_FF = 2816
N_MOD = 6
EPS = 1e-6

CHUNK = 32
NC_LAT = SEQ // CHUNK
NC_CTX = CTX_LEN // CHUNK
NC = NC_LAT + NC_CTX
CW = CHUNK * SSM_GROUP
SW = 4 * STATE
TAB = 128
LANES = 128
SUBLANES = 8

ROW_DIL = GRID_W // CHUNK
CONV_PAD = 32
CONV_ROWS = NC_LAT + 2 * CONV_PAD
CONV_PHASES = SUBLANES // ROW_DIL
CONV_BLOCK = 64
FF_BLOCK = 256
VMEM_LIMIT = 56 * 1024 * 1024


def _sigmoid(v):
    return 1.0 / (1.0 + jnp.exp(-v))


def _rms(v, g):
    ms = jnp.mean(v * v, axis=-1, keepdims=True)
    return v * lax.rsqrt(ms + EPS) * g


def _dot(a, b):
    return jnp.dot(a, b, preferred_element_type=F32)


def _dot_nt(a, b):
    return lax.dot_general(a, b, (((1,), (1,)), ((), ())), preferred_element_type=F32)


def _dot_tn(a, b):
    return lax.dot_general(a, b, (((0,), (0,)), ((), ())), preferred_element_type=F32)


def _row_gather(x_hbm, buf, sem, step, slot):
    return pltpu.make_async_copy(x_hbm.at[:, step, :], buf.at[slot], sem.at[slot])


def _row_scatter(buf, o_hbm, sem, step, slot):
    return pltpu.make_async_copy(buf.at[slot], o_hbm.at[:, step, :], sem.at[slot])


def _mod_kernel(c_ref, w_ref, b_ref, o_ref):
    c = c_ref[...]
    s = c * _sigmoid(c)
    o_ref[...] = jnp.dot(s, w_ref[...], precision=HI, preferred_element_type=F32) + b_ref[...]


def _mod(cvec, w_mod, b_mod):
    return pl.pallas_call(
        _mod_kernel,
        out_shape=jax.ShapeDtypeStruct((8, N_MOD * D_MODEL), F32),
        grid=(N_MOD,),
        in_specs=[
            pl.BlockSpec((8, D_MODEL), lambda j: (0, 0)),
            pl.BlockSpec((D_MODEL, D_MODEL), lambda j: (0, j)),
            pl.BlockSpec((1, D_MODEL), lambda j: (0, j)),
        ],
        out_specs=pl.BlockSpec((8, D_MODEL), lambda j: (0, j)),
        name="mod",
    )(cvec, w_mod, b_mod)


def _inproj_kernel(x_hbm, mod_ref, g_ref, wc_ref, wst_ref, a_ref, ut_ref, xbuf, sem):
    s = pl.program_id(0)
    n = pl.num_programs(0)
    slot = s % 2

    @pl.when(s == 0)
    def _():
        _row_gather(x_hbm, xbuf, sem, 0, 0).start()

    @pl.when(s + 1 < n)
    def _():
        _row_gather(x_hbm, xbuf, sem, s + 1, 1 - slot).start()

    _row_gather(x_hbm, xbuf, sem, s, slot).wait()

    h = _rms(xbuf[slot], g_ref[...])
    h = (h * (1.0 + mod_ref[1:2, :]) + mod_ref[0:1, :]).astype(BF16)
    z = _dot(h, wc_ref[...])
    a_ref[...] = z[:, :D_CONV] * _sigmoid(z[:, D_CONV:])
    zt = _dot_nt(wst_ref[...], h)
    ut_ref[...] = zt.reshape(N_GROUPS, SSM_GROUP, NC_LAT).astype(ut_ref.dtype)


def _inproj(x3, mods, norm_g, w_conv, w_ssm_t):
    const = lambda s: (0, 0)
    return pl.pallas_call(
        _inproj_kernel,
        out_shape=[jax.ShapeDtypeStruct((CHUNK, NC_LAT, D_CONV), F32),
                   jax.ShapeDtypeStruct((N_GROUPS, CW, NC_LAT), BF16)],
        grid=(CHUNK,),
        in_specs=[
            pl.BlockSpec(memory_space=pl.ANY),
            pl.BlockSpec((N_MOD, D_MODEL), const),
            pl.BlockSpec((1, D_MODEL), const),
            pl.BlockSpec((D_MODEL, 2 * D_CONV), const),
            pl.BlockSpec((D_SSM, D_MODEL), const),
        ],
        out_specs=[pl.BlockSpec((None, NC_LAT, D_CONV), lambda s: (s, 0, 0)),
                   pl.BlockSpec((N_GROUPS, SSM_GROUP, NC_LAT), lambda s: (0, s, 0))],
        scratch_shapes=[
            pltpu.VMEM((2, NC_LAT, D_MODEL), F32),
            pltpu.SemaphoreType.DMA((2,)),
        ],
        compiler_params=pltpu.CompilerParams(dimension_semantics=("arbitrary",),
                                             vmem_limit_bytes=VMEM_LIMIT),
        name="inproj",
    )(x3, mods, norm_g, w_conv, w_ssm_t)


def _ctxproj_kernel(x_ref, mod_ref, g_ref, w_ref, u_ref):
    h = _rms(x_ref[...], g_ref[...])
    h = (h * (1.0 + mod_ref[1:2, :]) + mod_ref[0:1, :]).astype(BF16)
    u_ref[...] = _dot(h, w_ref[...])


def _ctxproj(ctx2d, mods, norm_g, w_ssm):
    return pl.pallas_call(
        _ctxproj_kernel,
        out_shape=jax.ShapeDtypeStruct((CTX_LEN, D_SSM), F32),
        name="ctxproj",
    )(ctx2d, mods, norm_g, w_ssm)


def _split2(v):
    hi = v.astype(BF16)
    lo = (v - hi.astype(F32)).astype(BF16)
    return hi, lo


def _expand_right(table, sel):
    hi, lo = _split2(table)
    return _dot(hi, sel) + _dot(lo, sel)


def _expand_left(sel, table):
    hi, lo = _split2(table)
    return _dot(sel, hi) + _dot(sel, lo)


def _ssm_w_kernel(arow_ref, acol_ref, ldt_ref, b_ref, c_ref, d_ref,
                  wxt_ref, toept_ref, wct_ref, dec_ref):
    i32 = jnp.int32
    tau_l = jnp.minimum(lax.broadcasted_iota(i32, (1, TAB), 1), CHUNK).astype(F32)
    tau_r = jnp.minimum(lax.broadcasted_iota(i32, (TAB, 1), 0), CHUNK).astype(F32)
    tab_row = lax.broadcasted_iota(i32, (TAB, 1), 0)
    tab_lane = lax.broadcasted_iota(i32, (1, TAB), 1)

    lane_x = lax.broadcasted_iota(i32, (1, CW), 1)
    s_of_lane = lane_x >> GROUP_SHIFT
    lane_g = lax.broadcasted_iota(i32, (1, 2 * CW), 1)
    j_of_lane = lane_g >> GROUP_SHIFT
    t_of_row = lax.broadcasted_iota(i32, (CW, 1), 0) >> GROUP_SHIFT
    ch_row = lax.broadcasted_iota(i32, (SSM_GROUP, 1), 0)
    ci_sel = (ch_row == (lane_x & (SSM_GROUP - 1))).astype(BF16)

    g_lhs, g_rhs = [], []
    for d in range(2):
        dt = jnp.exp(ldt_ref[d])
        ar_r, ai_r = arow_ref[d, 0:1, :], arow_ref[d, 1:2, :]
        ar_c, ai_c = acol_ref[d, :, 0:1], acol_ref[d, :, 1:2]

        mag_c = jnp.exp(ar_c * dt * tau_l)
        ph_c = ai_c * dt * tau_l
        pc_re, pc_im = mag_c * jnp.cos(ph_c), mag_c * jnp.sin(ph_c)
        mag_r = jnp.exp(tau_r * (ar_r * dt))
        ph_r = tau_r * (ai_r * dt)
        pr_re, pr_im = mag_r * jnp.cos(ph_r), mag_r * jnp.sin(ph_r)

        n_re, n_im = pc_re[:, 1:2] - 1.0, pc_im[:, 1:2]
        den = ar_c * ar_c + ai_c * ai_c
        q_re = (n_re * ar_c + n_im * ai_c) / den
        q_im = (n_im * ar_c - n_re * ai_c) / den
        b_re, b_im = b_ref[d, 0], b_ref[d, 1]
        bb_re = q_re * b_re - q_im * b_im
        bb_im = q_re * b_im + q_im * b_re
        bt_re, bt_im = _expand_right(bb_re, ci_sel), _expand_right(bb_im, ci_sel)

        e_s = (CHUNK - 1 - s_of_lane) if d == 0 else s_of_lane
        sel_s = (tab_row == e_s).astype(BF16)
        px_re, px_im = _expand_right(pc_re, sel_s), _expand_right(pc_im, sel_s)
        wxt_ref[d * STATE:(d + 1) * STATE, :] = (px_re * bt_re - px_im * bt_im).astype(wxt_ref.dtype)
        wxt_ref[(2 + d) * STATE:(3 + d) * STATE, :] = (px_re * bt_im + px_im * bt_re).astype(wxt_ref.dtype)

        e_t = (t_of_row + 1) if d == 0 else (CHUNK - t_of_row)
        sel_t = (e_t == tab_lane).astype(BF16)
        py_re, py_im = _expand_left(sel_t, pr_re), _expand_left(sel_t, pr_im)
        c_re, c_im = c_ref[d, 0], c_ref[d, 1]
        ct_re = jnp.broadcast_to(c_re[None], (CHUNK, SSM_GROUP, STATE)).reshape(CW, STATE)
        ct_im = jnp.broadcast_to(c_im[None], (CHUNK, SSM_GROUP, STATE)).reshape(CW, STATE)
        wct_ref[:, d * STATE:(d + 1) * STATE] = (ct_re * py_re - ct_im * py_im).astype(wct_ref.dtype)
        wct_ref[:, (2 + d) * STATE:(3 + d) * STATE] = (-(ct_re * py_im + ct_im * py_re)).astype(wct_ref.dtype)

        e_j = ((CHUNK - 1) - j_of_lane) if d == 0 else (j_of_lane - (CHUNK - 1))
        e_j = jnp.where((e_j >= 0) & (e_j < CHUNK), e_j, -1)
        sel_j = (tab_row == e_j).astype(BF16)
        pg_re, pg_im = _expand_right(pc_re, sel_j), _expand_right(pc_im, sel_j)
        bg_re = jnp.concatenate([bt_re, bt_re], axis=1)
        bg_im = jnp.concatenate([bt_im, bt_im], axis=1)
        m_re = pg_re * bg_re - pg_im * bg_im
        m_im = pg_re * bg_im + pg_im * bg_re
        for cv, mv in ((c_re, m_re), (-c_im, m_im)):
            c_hi, c_lo = _split2(cv)
            m_hi, m_lo = _split2(mv)
            g_lhs += [c_hi, c_hi, c_lo]
            g_rhs += [m_hi, m_lo, m_hi]

        dec_ref[:, d * STATE:(d + 1) * STATE] = pr_re[CHUNK:CHUNK + 1, :]
        dec_ref[:, (2 + d) * STATE:(3 + d) * STATE] = pr_im[CHUNK:CHUNK + 1, :]

    g_pad = sum(_dot(lhs, rhs) for lhs, rhs in zip(g_lhs, g_rhs))
    on_diag = lane_g == (CHUNK - 1) * SSM_GROUP + ch_row
    g_pad = g_pad + jnp.where(on_diag, d_ref[...], 0.0)
    for t in range(CHUNK):
        k = (CHUNK - 1 - t) * SSM_GROUP
        rolled = g_pad if k == 0 else pltpu.roll(g_pad, 2 * CW - k, axis=1)
        toept_ref[t * SSM_GROUP:(t + 1) * SSM_GROUP, :] = rolled[:, :CW].astype(toept_ref.dtype)


def _ssm_weights(arow, acol, ldt, b, c, dcol):
    def spec(*tail):
        zeros = (0,) * len(tail)
        return pl.BlockSpec((None,) + tail, lambda g: (g,) + zeros)
    return pl.pallas_call(
        _ssm_w_kernel,
        out_shape=[
            jax.ShapeDtypeStruct((N_GROUPS, SW, CW), BF16),
            jax.ShapeDtypeStruct((N_GROUPS, CW, CW), BF16),
            jax.ShapeDtypeStruct((N_GROUPS, CW, SW), BF16),
            jax.ShapeDtypeStruct((N_GROUPS, 1, SW), F32),
        ],
        grid=(N_GROUPS,),
        in_specs=[spec(2, 2, STATE), spec(2, STATE, 2), spec(2, 1, 1),
                  spec(2, 2, STATE, SSM_GROUP), spec(2, 2, SSM_GROUP, STATE), spec(SSM_GROUP, 1)],
        out_specs=[spec(SW, CW), spec(CW, CW), spec(CW, SW), spec(1, SW)],
        name="ssm_w",
    )(arow, acol, ldt, b, c, dcol)


def _ssm_x_kernel(ut_ref, utc_ref, wxt_ref, x_ref):
    x_ref[0:NC_LAT, :] = _dot(wxt_ref[...], ut_ref[...]).T
    x_ref[NC_LAT:NC, :] = _dot(wxt_ref[...], utc_ref[...]).T[0:NC_CTX, :]


def _ssm_x(ut, ut_ctx, wxt):
    return pl.pallas_call(
        _ssm_x_kernel,
        out_shape=jax.ShapeDtypeStruct((N_GROUPS, NC, SW), F32),
        grid=(N_GROUPS,),
        in_specs=[pl.BlockSpec((None, CW, NC_LAT), lambda g: (g, 0, 0)),
                  pl.BlockSpec((None, CW, LANES), lambda g: (g, 0, 0)),
                  pl.BlockSpec((None, SW, CW), lambda g: (g, 0, 0))],
        out_specs=pl.BlockSpec((None, NC, SW), lambda g: (g, 0, 0)),
        name="ssm_x",
    )(ut, ut_ctx, wxt)


def _ssm_scan_kernel(x_ref, dec_ref, s_ref):
    half = 2 * STATE
    d_re = dec_ref[:, 0:half]
    d_im = dec_ref[:, half:2 * half]
    is_fwd = lax.broadcasted_iota(jnp.int32, (N_GROUPS, half), 1) < STATE

    def body(k, carry):
        s_re, s_im = carry
        cf = jnp.where(k < NC_CTX, NC_LAT + k, k - NC_CTX)
        cb = NC - 1 - k
        s_ref[cf, :, 0:STATE] = s_re[:, 0:STATE]
        s_ref[cf, :, half:half + STATE] = s_im[:, 0:STATE]
        s_ref[cb, :, STATE:half] = s_re[:, STATE:half]
        s_ref[cb, :, half + STATE:2 * half] = s_im[:, STATE:half]
        xf = x_ref[cf]
        xb = x_ref[cb]
        x_re = jnp.where(is_fwd, xf[:, 0:half], xb[:, 0:half])
        x_im = jnp.where(is_fwd, xf[:, half:], xb[:, half:])
        n_re = d_re * s_re - d_im * s_im + x_re
        n_im = d_re * s_im + d_im * s_re + x_im
        return n_re, n_im

    zero = jnp.zeros((N_GROUPS, half), F32)
    lax.fori_loop(0, NC, body, (zero, zero))


def _ssm_scan(x_t, dec):
    vmem = pl.BlockSpec(memory_space=pltpu.VMEM)
    return pl.pallas_call(
        _ssm_scan_kernel,
        out_shape=jax.ShapeDtypeStruct((NC, N_GROUPS, SW), F32),
        in_specs=[vmem, vmem],
        out_specs=vmem,
        compiler_params=pltpu.CompilerParams(vmem_limit_bytes=VMEM_LIMIT),
        name="ssm_scan",
    )(x_t, dec)


def _ssm_y_kernel(ut_ref, s_ref, toept_ref, wct_ref, yt_ref):
    y = _dot(toept_ref[...], ut_ref[...])
    y = y + _dot_nt(wct_ref[...], s_ref[...].astype(BF16))
    yt_ref[...] = y.astype(yt_ref.dtype)


def _ssm_y(ut, s_g, toept, wct):
    return pl.pallas_call(
        _ssm_y_kernel,
        out_shape=jax.ShapeDtypeStruct((N_GROUPS, CW, NC_LAT), F32),
        grid=(N_GROUPS,),
        in_specs=[pl.BlockSpec((None, CW, NC_LAT), lambda g: (g, 0, 0)),
                  pl.BlockSpec((None, NC_LAT, SW), lambda g: (g, 0, 0)),
                  pl.BlockSpec((None, CW, CW), lambda g: (g, 0, 0)),
                  pl.BlockSpec((None, CW, SW), lambda g: (g, 0, 0))],
        out_specs=pl.BlockSpec((None, CW, NC_LAT), lambda g: (g, 0, 0)),
        name="ssm_y",
    )(ut, s_g, toept, wct)


def _conv_ln_swish(a_ref, cw_ref, cb_ref, lg_ref, lb_ref, apad, ash, yc_scr):
    apad[CONV_PAD:CONV_PAD + NC_LAT, :] = a_ref[...]
    span = CONV_ROWS - SUBLANES
    for m in range(1, CONV_PHASES):
        ash[m - 1, 0:span, :] = apad[pl.ds(ROW_DIL * m, span), :]

    for rb in range(NC_LAT // CONV_BLOCK):
        r0 = rb * CONV_BLOCK
        pieces = []
        for lt in range(D_CONV // LANES):
            lanes = slice(lt * LANES, (lt + 1) * LANES)
            acc = None
            for k in range(CONV_WIDTH):
                off = CONV_PAD + ROW_DIL * (k - CONV_WIDTH // 2)
                m, q = (off // ROW_DIL) % CONV_PHASES, off // SUBLANES
                src = apad if m == 0 else ash.at[m - 1]
                r1 = r0 + SUBLANES * q
                term = src[r1:r1 + CONV_BLOCK, lanes] * cw_ref[k:k + 1, lanes]
                acc = term if acc is None else acc + term
            pieces.append(acc)
        y = jnp.concatenate(pieces, axis=1) + cb_ref[...]
        mu = jnp.mean(y, axis=-1, keepdims=True)
        yc = y - mu
        var = jnp.mean(yc * yc, axis=-1, keepdims=True)
        yn = yc * lax.rsqrt(var + EPS) * lg_ref[...] + lb_ref[...]
        yc_scr[r0:r0 + CONV_BLOCK, :] = (yn * _sigmoid(yn)).astype(yc_scr.dtype)


def _tail_kernel(x_hbm, a_ref, yt_ref, mod_ref, cw_ref, cb_ref, lg_ref, lb_ref, wglut_ref, bglu_ref,
                 wout_ref, n2g_ref, wgu_ref, wdn_ref, fg_ref, o_hbm,
                 xbuf, obuf, sem, osem, apad, ash, yc_scr, act_scr):
    t = pl.program_id(0)
    n = pl.num_programs(0)
    slot = t % 2

    @pl.when(t == 0)
    def _():
        _row_gather(x_hbm, xbuf, sem, 0, 0).start()
        zeros = jnp.zeros((CONV_PAD, D_CONV), F32)
        apad[0:CONV_PAD, :] = zeros
        apad[CONV_PAD + NC_LAT:CONV_ROWS, :] = zeros

    @pl.when(t + 1 < n)
    def _():
        _row_gather(x_hbm, xbuf, sem, t + 1, 1 - slot).start()

    @pl.when(t >= 2)
    def _():
        _row_scatter(obuf, o_hbm, osem, t - 2, slot).wait()

    _row_gather(x_hbm, xbuf, sem, t, slot).wait()

    g1 = mod_ref[2:3, :]
    sh2 = mod_ref[3:4, :]
    sc2 = mod_ref[4:5, :]
    g2 = mod_ref[5:6, :]

    _conv_ln_swish(a_ref, cw_ref, cb_ref, lg_ref, lb_ref, apad, ash, yc_scr)

    y_t = yt_ref[...].reshape(D_SSM, NC_LAT).astype(F32)
    gl = jax.nn.gelu(y_t, approximate=True)
    gate = _dot(wglut_ref[...], gl.astype(BF16)) + bglu_ref[...]
    yss_t = (gl * _sigmoid(gate)).astype(BF16)

    mix = _dot(yc_scr[...], wout_ref[0:D_CONV, :]) + _dot_tn(yss_t, wout_ref[D_CONV:, :])
    x1 = xbuf[slot] + g1 * mix
    h2 = (_rms(x1, n2g_ref[...]) * (1.0 + sc2) + sh2).astype(BF16)

    for j in range(D_FF // FF_BLOCK):
        cols = slice(j * FF_BLOCK, (j + 1) * FF_BLOCK)
        up_cols = slice(D_FF + j * FF_BLOCK, D_FF + (j + 1) * FF_BLOCK)
        gt = _dot(h2, wgu_ref[:, cols])
        up = _dot(h2, wgu_ref[:, up_cols])
        act_scr[:, cols] = (gt * _sigmoid(gt) * up).astype(act_scr.dtype)
    x2 = x1 + g2 * _dot(act_scr[...], wdn_ref[...])
    obuf[slot] = _rms(x2, fg_ref[...])

    _row_scatter(obuf, o_hbm, osem, t, slot).start()

    @pl.when(t == n - 1)
    def _():
        _row_scatter(obuf, o_hbm, osem, t - 1, 1 - slot).wait()
        _row_scatter(obuf, o_hbm, osem, t, slot).wait()


def _tail(x3, a, yt, mods, conv_w, conv_b, ln_g, ln_b, wglut, bglu, wout, n2g, wgu, wdn, fg):
    const = lambda t: (0, 0)

    def resident(shape):
        return pl.BlockSpec(shape, const, pipeline_mode=pl.Buffered(1))

    return pl.pallas_call(
        _tail_kernel,
        out_shape=jax.ShapeDtypeStruct((NC_LAT, CHUNK, D_MODEL), F32),
        grid=(CHUNK,),
        in_specs=[
            pl.BlockSpec(memory_space=pl.ANY),
            pl.BlockSpec((None, NC_LAT, D_CONV), lambda t: (t, 0, 0)),
            pl.BlockSpec((N_GROUPS, SSM_GROUP, NC_LAT), lambda t: (0, t, 0)),
            resident((N_MOD, D_MODEL)),
            resident((CONV_WIDTH, D_CONV)),
            resident((1, D_CONV)),
            resident((1, D_CONV)),
            resident((1, D_CONV)),
            resident((D_SSM, D_SSM)),
            resident((D_SSM, 1)),
            resident((D_MODEL, D_MODEL)),
            resident((1, D_MODEL)),
            resident((D_MODEL, 2 * D_FF)),
            resident((D_FF, D_MODEL)),
            resident((1, D_MODEL)),
        ],
        out_specs=pl.BlockSpec(memory_space=pl.ANY),
        scratch_shapes=[
            pltpu.VMEM((2, NC_LAT, D_MODEL), F32),
            pltpu.VMEM((2, NC_LAT, D_MODEL), F32),
            pltpu.SemaphoreType.DMA((2,)),
            pltpu.SemaphoreType.DMA((2,)),
            pltpu.VMEM((CONV_ROWS, D_CONV), F32),
            pltpu.VMEM((CONV_PHASES - 1, CONV_ROWS, D_CONV), F32),
            pltpu.VMEM((NC_LAT, D_CONV), BF16),
            pltpu.VMEM((NC_LAT, D_FF), BF16),
        ],
        compiler_params=pltpu.CompilerParams(dimension_semantics=("arbitrary",),
                                             vmem_limit_bytes=VMEM_LIMIT),
        name="tail",
    )(x3, a, yt, mods, conv_w, conv_b, ln_g, ln_b, wglut, bglu, wout, n2g, wgu, wdn, fg)


def kernel(x, c, ctx, c_ctx, w_mod, b_mod, norm1_g, w_in, conv_w, conv_b, conv_ln_g, conv_ln_b,
           ssm_a_re, ssm_a_im, ssm_log_dt, ssm_b_re, ssm_b_im, ssm_c_re, ssm_c_im, ssm_d,
           ssm_w_glu, ssm_b_glu, w_out, norm2_g, w_gate_up, w_down, final_norm_g):
    assert x.shape == (1, SEQ, D_MODEL) and ctx.shape == (1, CTX_LEN, D_MODEL)
    assert w_mod.shape[0] == 1, "single trunk layer"
    x3 = x.reshape(NC_LAT, CHUNK, D_MODEL)

    cvec = jnp.zeros((8, D_MODEL), F32).at[0].set(c[0]).at[1].set(c_ctx)
    mods_all = _mod(cvec, w_mod[0], b_mod[0][None, :])
    mods = mods_all[0].reshape(N_MOD, D_MODEL)
    mods_ctx = mods_all[1].reshape(N_MOD, D_MODEL)

    w_in_b = w_in[0].astype(BF16)
    w_ssm = w_in_b[:, 2 * D_CONV:]
    n1g = norm1_g[0][None, :]
    a, ut = _inproj(x3, mods, n1g, w_in_b[:, :2 * D_CONV], w_ssm.T)
    u_ctx = _ctxproj(ctx[0], mods_ctx, n1g, w_ssm)
    ut_ctx = u_ctx.reshape(NC_CTX, CHUNK, N_GROUPS, SSM_GROUP).transpose(2, 1, 3, 0)
    ut_ctx = jnp.pad(ut_ctx.reshape(N_GROUPS, CW, NC_CTX).astype(BF16),
                     ((0, 0), (0, 0), (0, LANES - NC_CTX)))

    a_re, a_im = ssm_a_re[0], ssm_a_im[0]
    arow = jnp.stack([a_re, a_im], axis=2).transpose(1, 0, 2, 3)
    acol = jnp.stack([a_re, a_im], axis=3).transpose(1, 0, 2, 3)
    ldt = ssm_log_dt[0].transpose(1, 0)[:, :, None, None]
    b_p = jnp.stack([ssm_b_re[0], ssm_b_im[0]], axis=2).transpose(1, 0, 2, 3, 4)
    c_p = jnp.stack([ssm_c_re[0], ssm_c_im[0]], axis=2).transpose(1, 0, 2, 3, 4)
    dcol = ssm_d[0].reshape(N_GROUPS, SSM_GROUP, 1)
    wxt, toept, wct, dec = _ssm_weights(arow, acol, ldt, b_p, c_p, dcol)

    x_st = _ssm_x(ut, ut_ctx, wxt)
    s_st = _ssm_scan(x_st.transpose(1, 0, 2), dec.reshape(N_GROUPS, SW))
    yt = _ssm_y(ut, s_st.transpose(1, 0, 2), toept, wct)

    out3 = _tail(x3, a, yt, mods, conv_w[0], conv_b[0][None, :], conv_ln_g[0][None, :],
                 conv_ln_b[0][None, :], ssm_w_glu[0].T.astype(BF16), ssm_b_glu[0][:, None],
                 w_out[0].astype(BF16), norm2_g[0][None, :], w_gate_up[0].astype(BF16),
                 w_down[0].astype(BF16), final_norm_g[None, :])
    return out3.reshape(1, SEQ, D_MODEL)
```

```python
import jax
import jax.numpy as jnp
from jax import lax
from jax.experimental import pallas as pl
from jax.experimental.pallas import tpu as pltpu

F32 = jnp.float32
BF16 = jnp.bfloat16

D_MODEL = 1024
SEQ = 16384
GRID_W = 64
CTX_LEN = 256
D_CONV = 512
D_SSM = 512
SSM_GROUP = 16
GROUP_SHIFT = 4
N_GROUPS = 32
STATE = 64
CONV_WIDTH = 31
D_FF = 2816
N_MOD = 6
EPS = 1e-6

CHUNK = 32
NC_LAT = SEQ // CHUNK
NC_CTX = CTX_LEN // CHUNK
NC = NC_LAT + NC_CTX
CW = CHUNK * SSM_GROUP
SW = 4 * STATE
TAB = 128
LANES = 128
SUBLANES = 8

ROW_DIL = GRID_W // CHUNK
CONV_PAD = 32
CONV_ROWS = NC_LAT + 2 * CONV_PAD
CONV_PHASES = SUBLANES // ROW_DIL
CONV_BLOCK = 64
FF_BLOCK = 256
VMEM_LIMIT = 56 * 1024 * 1024


def _sigmoid(v):
    return 1.0 / (1.0 + jnp.exp(-v))


def _rms(v, g):
    ms = jnp.mean(v * v, axis=-1, keepdims=True)
    return v * lax.rsqrt(ms + EPS) * g


def _dot(a, b):
    return jnp.dot(a, b, preferred_element_type=F32)


def _dot_nt(a, b):
    return lax.dot_general(a, b, (((1,), (1,)), ((), ())), preferred_element_type=F32)


def _dot_tn(a, b):
    return lax.dot_general(a, b, (((0,), (0,)), ((), ())), preferred_element_type=F32)


def _row_gather(x_hbm, buf, sem, step, slot):
    return pltpu.make_async_copy(x_hbm.at[:, step, :], buf.at[slot], sem.at[slot])


def _row_scatter(buf, o_hbm, sem, step, slot):
    return pltpu.make_async_copy(buf.at[slot], o_hbm.at[:, step, :], sem.at[slot])


def _mod_kernel(c_ref, w_ref, b_ref, o_ref):
    c = c_ref[...]
    s = c * _sigmoid(c)
    w = w_ref[...]
    for v in range(2):
        o_ref[v:v + 1, :] = jnp.sum(s[:, v:v + 1] * w, axis=0, keepdims=True) + b_ref[...]


def _mod(ccols, w_mod, b_mod):
    return pl.pallas_call(
        _mod_kernel,
        out_shape=jax.ShapeDtypeStruct((2, N_MOD * D_MODEL), F32),
        grid=(N_MOD,),
        in_specs=[
            pl.BlockSpec((D_MODEL, 2), lambda j: (0, 0)),
            pl.BlockSpec((D_MODEL, D_MODEL), lambda j: (0, j)),
            pl.BlockSpec((1, D_MODEL), lambda j: (0, j)),
        ],
        out_specs=pl.BlockSpec((2, D_MODEL), lambda j: (0, j)),
        name="mod",
    )(ccols, w_mod, b_mod)


def _inproj_kernel(x_hbm, mod_ref, g_ref, wc_ref, wst_ref, a_ref, ut_ref, xbuf, sem):
    s = pl.program_id(0)
    n = pl.num_programs(0)
    slot = s % 2

    @pl.when(s == 0)
    def _():
        _row_gather(x_hbm, xbuf, sem, 0, 0).start()

    @pl.when(s + 1 < n)
    def _():
        _row_gather(x_hbm, xbuf, sem, s + 1, 1 - slot).start()

    _row_gather(x_hbm, xbuf, sem, s, slot).wait()

    h = _rms(xbuf[slot], g_ref[...])
    h = (h * (1.0 + mod_ref[1:2, :]) + mod_ref[0:1, :]).astype(BF16)
    z = _dot(h, wc_ref[...])
    a_ref[...] = z[:, :D_CONV] * _sigmoid(z[:, D_CONV:])
    zt = _dot_nt(wst_ref[...], h)
    ut_ref[...] = zt.reshape(N_GROUPS, SSM_GROUP, NC_LAT).astype(ut_ref.dtype)


def _inproj(x3, mods, norm_g, w_conv, w_ssm_t):
    const = lambda s: (0, 0)
    return pl.pallas_call(
        _inproj_kernel,
        out_shape=[jax.ShapeDtypeStruct((CHUNK, NC_LAT, D_CONV), F32),
                   jax.ShapeDtypeStruct((N_GROUPS, CW, NC_LAT), BF16)],
        grid=(CHUNK,),
        in_specs=[
            pl.BlockSpec(memory_space=pl.ANY),
            pl.BlockSpec((N_MOD, D_MODEL), const),
            pl.BlockSpec((1, D_MODEL), const),
            pl.BlockSpec((D_MODEL, 2 * D_CONV), const),
            pl.BlockSpec((D_SSM, D_MODEL), const),
        ],
        out_specs=[pl.BlockSpec((None, NC_LAT, D_CONV), lambda s: (s, 0, 0)),
                   pl.BlockSpec((N_GROUPS, SSM_GROUP, NC_LAT), lambda s: (0, s, 0))],
        scratch_shapes=[
            pltpu.VMEM((2, NC_LAT, D_MODEL), F32),
            pltpu.SemaphoreType.DMA((2,)),
        ],
        compiler_params=pltpu.CompilerParams(dimension_semantics=("arbitrary",),
                                             vmem_limit_bytes=VMEM_LIMIT),
        name="inproj",
    )(x3, mods, norm_g, w_conv, w_ssm_t)


def _ctxproj_kernel(x_ref, mod_ref, g_ref, w_ref, u_ref):
    h = _rms(x_ref[...], g_ref[...])
    h = (h * (1.0 + mod_ref[1:2, :]) + mod_ref[0:1, :]).astype(BF16)
    u_ref[...] = _dot(h, w_ref[...])


def _ctxproj(ctx2d, mods, norm_g, w_ssm):
    return pl.pallas_call(
        _ctxproj_kernel,
        out_shape=jax.ShapeDtypeStruct((CTX_LEN, D_SSM), F32),
        name="ctxproj",
    )(ctx2d, mods, norm_g, w_ssm)


def _split2(v):
    hi = v.astype(BF16)
    lo = (v - hi.astype(F32)).astype(BF16)
    return hi, lo


def _expand_right(table, sel2):
    return _dot(jnp.concatenate(_split2(table), axis=1), sel2)


def _expand_left(sel2, table):
    return _dot(sel2, jnp.concatenate(_split2(table), axis=0))


def _ssm_selectors():
    i32 = jnp.int32
    tab = jnp.arange(TAB, dtype=i32)
    s_of = jnp.arange(CW, dtype=i32) >> GROUP_SHIFT
    j_of = jnp.arange(2 * CW, dtype=i32) >> GROUP_SHIFT
    e_t = jnp.stack([s_of + 1, CHUNK - s_of])
    e_j = jnp.stack([(CHUNK - 1) - j_of, j_of - (CHUNK - 1)])
    e_j = jnp.where((e_j >= 0) & (e_j < CHUNK), e_j, -1)
    sel_sb = (tab[:, None] == s_of[None, :]).astype(BF16)
    sel_t = (e_t[:, :, None] == tab[None, None, :]).astype(BF16)
    sel_j = (tab[None, :, None] == e_j[:, None, :]).astype(BF16)
    ci_sel = (jnp.arange(SSM_GROUP, dtype=i32)[:, None]
              == (jnp.arange(CW, dtype=i32) & (SSM_GROUP - 1))[None, :]).astype(BF16)
    return (jnp.concatenate([sel_sb, sel_sb], axis=0), jnp.concatenate([sel_t, sel_t], axis=2),
            jnp.concatenate([sel_j, sel_j], axis=1), ci_sel)


def _ssm_w_kernel(arow_ref, acol_ref, ldt_ref, b_ref, c_ref, d_ref, selsb_ref, selt_ref, selj_ref,
                  cisel_ref, wxt_ref, toept_ref, wct_ref, dec_ref):
    i32 = jnp.int32
    tau_l = jnp.minimum(lax.broadcasted_iota(i32, (1, TAB), 1), CHUNK).astype(F32)
    tau_r = jnp.minimum(lax.broadcasted_iota(i32, (TAB, 1), 0), CHUNK).astype(F32)
    lane_g = lax.broadcasted_iota(i32, (1, 2 * CW), 1)
    ch_row = lax.broadcasted_iota(i32, (SSM_GROUP, 1), 0)
    ci_sel = cisel_ref[...]

    g_lhs, g_rhs = [], []
    for d in range(2):
        dt = jnp.exp(ldt_ref[d])
        ar_r, ai_r = arow_ref[d, 0:1, :], arow_ref[d, 1:2, :]
        ar_c, ai_c = acol_ref[d, :, 0:1], acol_ref[d, :, 1:2]

        mag_c = jnp.exp(ar_c * dt * tau_l)
        ph_c = ai_c * dt * tau_l
        pc_re, pc_im = mag_c * jnp.cos(ph_c), mag_c * jnp.sin(ph_c)
        mag_r = jnp.exp(tau_r * (ar_r * dt))
        ph_r = tau_r * (ai_r * dt)
        pr_re, pr_im = mag_r * jnp.cos(ph_r), mag_r * jnp.sin(ph_r)

        n_re, n_im = pc_re[:, 1:2] - 1.0, pc_im[:, 1:2]
        den = ar_c * ar_c + ai_c * ai_c
        q_re = (n_re * ar_c + n_im * ai_c) / den
        q_im = (n_im * ar_c - n_re * ai_c) / den
        b_re, b_im = b_ref[d, 0], b_ref[d, 1]
        bb_re = q_re * b_re - q_im * b_im
        bb_im = q_re * b_im + q_im * b_re
        bt_re = sum(_dot(piece, ci_sel) for piece in _split2(bb_re))
        bt_im = sum(_dot(piece, ci_sel) for piece in _split2(bb_im))

        pg_re, pg_im = _expand_right(pc_re, selj_ref[d]), _expand_right(pc_im, selj_ref[d])
        bg_re = jnp.concatenate([bt_re, bt_re], axis=1)
        bg_im = jnp.concatenate([bt_im, bt_im], axis=1)
        m_re = pg_re * bg_re - pg_im * bg_im
        m_im = pg_re * bg_im + pg_im * bg_re
        c_re, c_im = c_ref[d, 0], c_ref[d, 1]
        for cv, mv in ((c_re, m_re), (-c_im, m_im)):
            c_hi, c_lo = _split2(cv)
            m_hi, m_lo = _split2(mv)
            g_lhs += [c_hi, c_hi, c_lo]
            g_rhs += [m_hi, m_lo, m_hi]

        if d == 0:
            px_re, px_im = pg_re[:, :CW], pg_im[:, :CW]
        else:
            px_re, px_im = _expand_right(pc_re, selsb_ref[...]), _expand_right(pc_im, selsb_ref[...])
        wxt_ref[d * STATE:(d + 1) * STATE, :] = (px_re * bt_re - px_im * bt_im).astype(wxt_ref.dtype)
        wxt_ref[(2 + d) * STATE:(3 + d) * STATE, :] = (px_re * bt_im + px_im * bt_re).astype(wxt_ref.dtype)

        py_re, py_im = _expand_left(selt_ref[d], pr_re), _expand_left(selt_ref[d], pr_im)
        ct_re = jnp.broadcast_to(c_re[None], (CHUNK, SSM_GROUP, STATE)).reshape(CW, STATE)
        ct_im = jnp.broadcast_to(c_im[None], (CHUNK, SSM_GROUP, STATE)).reshape(CW, STATE)
        wct_ref[:, d * STATE:(d + 1) * STATE] = (ct_re * py_re - ct_im * py_im).astype(wct_ref.dtype)
        wct_ref[:, (2 + d) * STATE:(3 + d) * STATE] = (-(ct_re * py_im + ct_im * py_re)).astype(wct_ref.dtype)

        dec_ref[:, d * STATE:(d + 1) * STATE] = pr_re[CHUNK:CHUNK + 1, :]
        dec_ref[:, (2 + d) * STATE:(3 + d) * STATE] = pr_im[CHUNK:CHUNK + 1, :]

    g_lhs = jnp.concatenate([piece.astype(F32) for piece in g_lhs], axis=1).astype(BF16)
    g_pad = _dot(g_lhs, jnp.concatenate(g_rhs, axis=0))
    on_diag = lane_g == (CHUNK - 1) * SSM_GROUP + ch_row
    g_pad = g_pad + jnp.where(on_diag, d_ref[...], 0.0)
    for t in range(CHUNK):
        k = (CHUNK - 1 - t) * SSM_GROUP
        rolled = g_pad if k == 0 else pltpu.roll(g_pad, 2 * CW - k, axis=1)
        toept_ref[t * SSM_GROUP:(t + 1) * SSM_GROUP, :] = rolled[:, :CW].astype(toept_ref.dtype)


def _ssm_weights(arow, acol, ldt, b, c, dcol):
    def spec(*tail):
        zeros = (0,) * len(tail)
        return pl.BlockSpec((None,) + tail, lambda g: (g,) + zeros)

    def const(arr):
        zeros = (0,) * arr.ndim
        return pl.BlockSpec(arr.shape, lambda g: zeros)

    sels = _ssm_selectors()
    return pl.pallas_call(
        _ssm_w_kernel,
        out_shape=[
            jax.ShapeDtypeStruct((N_GROUPS, SW, CW), BF16),
            jax.ShapeDtypeStruct((N_GROUPS, CW, CW), BF16),
            jax.ShapeDtypeStruct((N_GROUPS, CW, SW), BF16),
            jax.ShapeDtypeStruct((N_GROUPS, 1, SW), F32),
        ],
        grid=(N_GROUPS,),
        in_specs=[spec(2, 2, STATE), spec(2, STATE, 2), spec(2, 1, 1),
                  spec(2, 2, STATE, SSM_GROUP), spec(2, 2, SSM_GROUP, STATE), spec(SSM_GROUP, 1)]
                 + [const(arr) for arr in sels],
        out_specs=[spec(SW, CW), spec(CW, CW), spec(CW, SW), spec(1, SW)],
        name="ssm_w",
    )(arow, acol, ldt, b, c, dcol, *sels)


def _ssm_x_kernel(ut_ref, utc_ref, wxt_ref, x_ref):
    x_ref[0:NC_LAT, :] = _dot(wxt_ref[...], ut_ref[...]).T
    x_ref[NC_LAT:NC, :] = _dot(wxt_ref[...], utc_ref[...]).T[0:NC_CTX, :]


def _ssm_x(ut, ut_ctx, wxt):
    return pl.pallas_call(
        _ssm_x_kernel,
        out_shape=jax.ShapeDtypeStruct((N_GROUPS, NC, SW), F32),
        grid=(N_GROUPS,),
        in_specs=[pl.BlockSpec((None, CW, NC_LAT), lambda g: (g, 0, 0)),
                  pl.BlockSpec((None, CW, LANES), lambda g: (g, 0, 0)),
                  pl.BlockSpec((None, SW, CW), lambda g: (g, 0, 0))],
        out_specs=pl.BlockSpec((None, NC, SW), lambda g: (g, 0, 0)),
        name="ssm_x",
    )(ut, ut_ctx, wxt)


def _ssm_scan_kernel(x_ref, dec_ref, s_ref):
    half = 2 * STATE
    d_re = dec_ref[:, 0:half]
    d_im = dec_ref[:, half:2 * half]
    is_fwd = lax.broadcasted_iota(jnp.int32, (N_GROUPS, half), 1) < STATE

    def body(k, carry):
        s_re, s_im = carry
        cf = jnp.where(k < NC_CTX, NC_LAT + k, k - NC_CTX)
        cb = NC - 1 - k
        s_ref[cf, :, 0:STATE] = s_re[:, 0:STATE]
        s_ref[cf, :, half:half + STATE] = s_im[:, 0:STATE]
        s_ref[cb, :, STATE:half] = s_re[:, STATE:half]
        s_ref[cb, :, half + STATE:2 * half] = s_im[:, STATE:half]
        xf = x_ref[cf]
        xb = x_ref[cb]
        x_re = jnp.where(is_fwd, xf[:, 0:half], xb[:, 0:half])
        x_im = jnp.where(is_fwd, xf[:, half:], xb[:, half:])
        n_re = d_re * s_re - d_im * s_im + x_re
        n_im = d_re * s_im + d_im * s_re + x_im
        return n_re, n_im

    zero = jnp.zeros((N_GROUPS, half), F32)
    lax.fori_loop(0, NC, body, (zero, zero))


def _ssm_scan(x_t, dec):
    vmem = pl.BlockSpec(memory_space=pltpu.VMEM)
    return pl.pallas_call(
        _ssm_scan_kernel,
        out_shape=jax.ShapeDtypeStruct((NC, N_GROUPS, SW), F32),
        in_specs=[vmem, vmem],
        out_specs=vmem,
        compiler_params=pltpu.CompilerParams(vmem_limit_bytes=VMEM_LIMIT),
        name="ssm_scan",
    )(x_t, dec)


def _ssm_y_kernel(ut_ref, s_ref, toept_ref, wct_ref, yt_ref):
    y = _dot(toept_ref[...], ut_ref[...])
    y = y + _dot_nt(wct_ref[...], s_ref[...].astype(BF16))
    yt_ref[...] = y.astype(yt_ref.dtype)


def _ssm_y(ut, s_g, toept, wct):
    return pl.pallas_call(
        _ssm_y_kernel,
        out_shape=jax.ShapeDtypeStruct((N_GROUPS, CW, NC_LAT), BF16),
        grid=(N_GROUPS,),
        in_specs=[pl.BlockSpec((None, CW, NC_LAT), lambda g: (g, 0, 0)),
                  pl.BlockSpec((None, NC_LAT, SW), lambda g: (g, 0, 0)),
                  pl.BlockSpec((None, CW, CW), lambda g: (g, 0, 0)),
                  pl.BlockSpec((None, CW, SW), lambda g: (g, 0, 0))],
        out_specs=pl.BlockSpec((None, CW, NC_LAT), lambda g: (g, 0, 0)),
        name="ssm_y",
    )(ut, s_g, toept, wct)


def _conv_ln_swish(a_ref, cw_ref, cb_ref, lg_ref, lb_ref, apad, ash, yc_scr):
    apad[CONV_PAD:CONV_PAD + NC_LAT, :] = a_ref[...]
    span = CONV_ROWS - SUBLANES
    for m in range(1, CONV_PHASES):
        ash[m - 1, 0:span, :] = apad[pl.ds(ROW_DIL * m, span), :]

    for rb in range(NC_LAT // CONV_BLOCK):
        r0 = rb * CONV_BLOCK
        pieces = []
        for lt in range(D_CONV // LANES):
            lanes = slice(lt * LANES, (lt + 1) * LANES)
            acc = None
            for k in range(CONV_WIDTH):
                off = CONV_PAD + ROW_DIL * (k - CONV_WIDTH // 2)
                m, q = (off // ROW_DIL) % CONV_PHASES, off // SUBLANES
                src = apad if m == 0 else ash.at[m - 1]
                r1 = r0 + SUBLANES * q
                term = src[r1:r1 + CONV_BLOCK, lanes] * cw_ref[k:k + 1, lanes]
                acc = term if acc is None else acc + term
            pieces.append(acc)
        y = jnp.concatenate(pieces, axis=1) + cb_ref[...]
        mu = jnp.mean(y, axis=-1, keepdims=True)
        yc = y - mu
        var = jnp.mean(yc * yc, axis=-1, keepdims=True)
        yn = yc * lax.rsqrt(var + EPS) * lg_ref[...] + lb_ref[...]
        yc_scr[r0:r0 + CONV_BLOCK, :] = (yn * _sigmoid(yn)).astype(yc_scr.dtype)


def _tail_kernel(x_hbm, a_ref, yt_ref, mod_ref, cw_ref, cb_ref, lg_ref, lb_ref, wglut_ref, bglu_ref,
                 wout_ref, n2g_ref, wgu_ref, wdn_ref, fg_ref, o_hbm,
                 xbuf, obuf, sem, osem, apad, ash, yc_scr, act_scr):
    t = pl.program_id(0)
    n = pl.num_programs(0)
    slot = t % 2

    @pl.when(t == 0)
    def _():
        _row_gather(x_hbm, xbuf, sem, 0, 0).start()
        zeros = jnp.zeros((CONV_PAD, D_CONV), F32)
        apad[0:CONV_PAD, :] = zeros
        apad[CONV_PAD + NC_LAT:CONV_ROWS, :] = zeros

    @pl.when(t + 1 < n)
    def _():
        _row_gather(x_hbm, xbuf, sem, t + 1, 1 - slot).start()

    @pl.when(t >= 2)
    def _():
        _row_scatter(obuf, o_hbm, osem, t - 2, slot).wait()

    _row_gather(x_hbm, xbuf, sem, t, slot).wait()

    g1 = mod_ref[2:3, :]
    sh2 = mod_ref[3:4, :]
    sc2 = mod_ref[4:5, :]
    g2 = mod_ref[5:6, :]

    _conv_ln_swish(a_ref, cw_ref, cb_ref, lg_ref, lb_ref, apad, ash, yc_scr)

    y_t = yt_ref[...].reshape(D_SSM, NC_LAT).astype(F32)
    gl = jax.nn.gelu(y_t, approximate=True)
    gate = _dot(wglut_ref[...], gl.astype(BF16)) + bglu_ref[...]
    yss_t = (gl * _sigmoid(gate)).astype(BF16)

    mix = _dot(yc_scr[...], wout_ref[0:D_CONV, :]) + _dot_tn(yss_t, wout_ref[D_CONV:, :])
    x1 = xbuf[slot] + g1 * mix
    h2 = (_rms(x1, n2g_ref[...]) * (1.0 + sc2) + sh2).astype(BF16)

    for j in range(D_FF // FF_BLOCK):
        cols = slice(j * FF_BLOCK, (j + 1) * FF_BLOCK)
        up_cols = slice(D_FF + j * FF_BLOCK, D_FF + (j + 1) * FF_BLOCK)
        gt = _dot(h2, wgu_ref[:, cols])
        up = _dot(h2, wgu_ref[:, up_cols])
        act_scr[:, cols] = (gt * _sigmoid(gt) * up).astype(act_scr.dtype)
    x2 = x1 + g2 * _dot(act_scr[...], wdn_ref[...])
    obuf[slot] = _rms(x2, fg_ref[...])

    _row_scatter(obuf, o_hbm, osem, t, slot).start()

    @pl.when(t == n - 1)
    def _():
        _row_scatter(obuf, o_hbm, osem, t - 1, 1 - slot).wait()
        _row_scatter(obuf, o_hbm, osem, t, slot).wait()


def _tail(x3, a, yt, mods, conv_w, conv_b, ln_g, ln_b, wglut, bglu, wout, n2g, wgu, wdn, fg):
    const = lambda t: (0, 0)

    def resident(shape):
        return pl.BlockSpec(shape, const, pipeline_mode=pl.Buffered(1))

    return pl.pallas_call(
        _tail_kernel,
        out_shape=jax.ShapeDtypeStruct((NC_LAT, CHUNK, D_MODEL), F32),
        grid=(CHUNK,),
        in_specs=[
            pl.BlockSpec(memory_space=pl.ANY),
            pl.BlockSpec((None, NC_LAT, D_CONV), lambda t: (t, 0, 0)),
            pl.BlockSpec((N_GROUPS, SSM_GROUP, NC_LAT), lambda t: (0, t, 0)),
            resident((N_MOD, D_MODEL)),
            resident((CONV_WIDTH, D_CONV)),
            resident((1, D_CONV)),
            resident((1, D_CONV)),
            resident((1, D_CONV)),
            resident((D_SSM, D_SSM)),
            resident((D_SSM, 1)),
            resident((D_MODEL, D_MODEL)),
            resident((1, D_MODEL)),
            resident((D_MODEL, 2 * D_FF)),
            resident((D_FF, D_MODEL)),
            resident((1, D_MODEL)),
        ],
        out_specs=pl.BlockSpec(memory_space=pl.ANY),
        scratch_shapes=[
            pltpu.VMEM((2, NC_LAT, D_MODEL), F32),
            pltpu.VMEM((2, NC_LAT, D_MODEL), F32),
            pltpu.SemaphoreType.DMA((2,)),
            pltpu.SemaphoreType.DMA((2,)),
            pltpu.VMEM((CONV_ROWS, D_CONV), F32),
            pltpu.VMEM((CONV_PHASES - 1, CONV_ROWS, D_CONV), F32),
            pltpu.VMEM((NC_LAT, D_CONV), BF16),
            pltpu.VMEM((NC_LAT, D_FF), BF16),
        ],
        compiler_params=pltpu.CompilerParams(dimension_semantics=("arbitrary",),
                                             vmem_limit_bytes=VMEM_LIMIT),
        name="tail",
    )(x3, a, yt, mods, conv_w, conv_b, ln_g, ln_b, wglut, bglu, wout, n2g, wgu, wdn, fg)


def kernel(x, c, ctx, c_ctx, w_mod, b_mod, norm1_g, w_in, conv_w, conv_b, conv_ln_g, conv_ln_b,
           ssm_a_re, ssm_a_im, ssm_log_dt, ssm_b_re, ssm_b_im, ssm_c_re, ssm_c_im, ssm_d,
           ssm_w_glu, ssm_b_glu, w_out, norm2_g, w_gate_up, w_down, final_norm_g):
    assert x.shape == (1, SEQ, D_MODEL) and ctx.shape == (1, CTX_LEN, D_MODEL)
    assert w_mod.shape[0] == 1, "single trunk layer"
    x3 = x.reshape(NC_LAT, CHUNK, D_MODEL)

    mods_all = _mod(jnp.stack([c[0], c_ctx], axis=1), w_mod[0], b_mod[0][None, :])
    mods = mods_all[0].reshape(N_MOD, D_MODEL)
    mods_ctx = mods_all[1].reshape(N_MOD, D_MODEL)

    w_in_b = w_in[0].astype(BF16)
    w_ssm = w_in_b[:, 2 * D_CONV:]
    n1g = norm1_g[0][None, :]
    a, ut = _inproj(x3, mods, n1g, w_in_b[:, :2 * D_CONV], w_ssm.T)
    u_ctx = _ctxproj(ctx[0], mods_ctx, n1g, w_ssm)
    ut_ctx = u_ctx.reshape(NC_CTX, CHUNK, N_GROUPS, SSM_GROUP).transpose(2, 1, 3, 0)
    ut_ctx = jnp.pad(ut_ctx.reshape(N_GROUPS, CW, NC_CTX).astype(BF16),
                     ((0, 0), (0, 0), (0, LANES - NC_CTX)))

    a_re, a_im = ssm_a_re[0], ssm_a_im[0]
    arow = jnp.stack([a_re, a_im], axis=2).transpose(1, 0, 2, 3)
    acol = jnp.stack([a_re, a_im], axis=3).transpose(1, 0, 2, 3)
    ldt = ssm_log_dt[0].transpose(1, 0)[:, :, None, None]
    b_p = jnp.stack([ssm_b_re[0], ssm_b_im[0]], axis=2).transpose(1, 0, 2, 3, 4)
    c_p = jnp.stack([ssm_c_re[0], ssm_c_im[0]], axis=2).transpose(1, 0, 2, 3, 4)
    dcol = ssm_d[0].reshape(N_GROUPS, SSM_GROUP, 1)
    wxt, toept, wct, dec = _ssm_weights(arow, acol, ldt, b_p, c_p, dcol)

    x_st = _ssm_x(ut, ut_ctx, wxt)
    s_st = _ssm_scan(x_st.transpose(1, 0, 2), dec.reshape(N_GROUPS, SW))
    yt = _ssm_y(ut, s_st.transpose(1, 0, 2), toept, wct)

    out3 = _tail(x3, a, yt, mods, conv_w[0], conv_b[0][None, :], conv_ln_g[0][None, :],
                 conv_ln_b[0][None, :], ssm_w_glu[0].T.astype(BF16), ssm_b_glu[0][:, None],
                 w_out[0].astype(BF16), norm2_g[0][None, :], w_gate_up[0].astype(BF16),
                 w_down[0].astype(BF16), final_norm_g[None, :])
    return out3.reshape(1, SEQ, D_MODEL)
```

```python
import jax
import jax.numpy as jnp
from jax import lax
from jax.experimental import pallas as pl
from jax.experimental.pallas import tpu as pltpu

F32 = jnp.float32
BF16 = jnp.bfloat16

D_MODEL = 1024
SEQ = 16384
GRID_W = 64
CTX_LEN = 256
D_CONV = 512
D_SSM = 512
SSM_GROUP = 16
GROUP_SHIFT = 4
N_GROUPS = 32
STATE = 64
CONV_WIDTH = 31
D_FF = 2816
N_MOD = 6
EPS = 1e-6

CHUNK = 32
NC_LAT = SEQ // CHUNK
NC_CTX = CTX_LEN // CHUNK
NC = NC_LAT + NC_CTX
CW = CHUNK * SSM_GROUP
SW = 4 * STATE
TAB = 128
LANES = 128
SUBLANES = 8

ROW_DIL = GRID_W // CHUNK
CONV_PAD = 32
CONV_ROWS = NC_LAT + 2 * CONV_PAD
CONV_PHASES = SUBLANES // ROW_DIL
CONV_BLOCK = 64
SCAN_GROUPS = 8
FF_BLOCK = 256
VMEM_LIMIT = 56 * 1024 * 1024


def _sigmoid(v):
    return 1.0 / (1.0 + jnp.exp(-v))


def _rms(v, g):
    ms = jnp.mean(v * v, axis=-1, keepdims=True)
    return v * lax.rsqrt(ms + EPS) * g


def _dot(a, b):
    return jnp.dot(a, b, preferred_element_type=F32)


def _dot_nt(a, b):
    return lax.dot_general(a, b, (((1,), (1,)), ((), ())), preferred_element_type=F32)


def _dot_tn(a, b):
    return lax.dot_general(a, b, (((0,), (0,)), ((), ())), preferred_element_type=F32)


def _row_gather(x_hbm, buf, sem, step, slot):
    return pltpu.make_async_copy(x_hbm.at[:, step, :], buf.at[slot], sem.at[slot])


def _row_scatter(buf, o_hbm, sem, step, slot):
    return pltpu.make_async_copy(buf.at[slot], o_hbm.at[:, step, :], sem.at[slot])


def _mod_kernel(c_ref, w_ref, b_ref, o_ref):
    c = c_ref[...]
    s = c * _sigmoid(c)
    w = w_ref[...]
    for v in range(2):
        o_ref[v:v + 1, :] = jnp.sum(s[:, v:v + 1] * w, axis=0, keepdims=True) + b_ref[...]


def _mod(ccols, w_mod, b_mod):
    return pl.pallas_call(
        _mod_kernel,
        out_shape=jax.ShapeDtypeStruct((2, N_MOD * D_MODEL), F32),
        grid=(N_MOD,),
        in_specs=[
            pl.BlockSpec((D_MODEL, 2), lambda j: (0, 0)),
            pl.BlockSpec((D_MODEL, D_MODEL), lambda j: (0, j)),
            pl.BlockSpec((1, D_MODEL), lambda j: (0, j)),
        ],
        out_specs=pl.BlockSpec((2, D_MODEL), lambda j: (0, j)),
        name="mod",
    )(ccols, w_mod, b_mod)


def _inproj_kernel(x_hbm, mod_ref, g_ref, wc_ref, wst_ref, a_ref, ut_ref, xbuf, sem):
    s = pl.program_id(0)
    n = pl.num_programs(0)
    slot = s % 2

    @pl.when(s == 0)
    def _():
        _row_gather(x_hbm, xbuf, sem, 0, 0).start()

    @pl.when(s + 1 < n)
    def _():
        _row_gather(x_hbm, xbuf, sem, s + 1, 1 - slot).start()

    _row_gather(x_hbm, xbuf, sem, s, slot).wait()

    h = _rms(xbuf[slot], g_ref[...])
    h = (h * (1.0 + mod_ref[1:2, :]) + mod_ref[0:1, :]).astype(BF16)
    z = _dot(h, wc_ref[...])
    a_ref[...] = z[:, :D_CONV] * _sigmoid(z[:, D_CONV:])
    zt = _dot_nt(wst_ref[...], h)
    ut_ref[...] = zt.reshape(N_GROUPS, SSM_GROUP, NC_LAT).astype(ut_ref.dtype)


def _inproj(x3, mods, norm_g, w_conv, w_ssm_t):
    const = lambda s: (0, 0)
    return pl.pallas_call(
        _inproj_kernel,
        out_shape=[jax.ShapeDtypeStruct((CHUNK, NC_LAT, D_CONV), F32),
                   jax.ShapeDtypeStruct((N_GROUPS, CW, NC_LAT), BF16)],
        grid=(CHUNK,),
        in_specs=[
            pl.BlockSpec(memory_space=pl.ANY),
            pl.BlockSpec((N_MOD, D_MODEL), const),
            pl.BlockSpec((1, D_MODEL), const),
            pl.BlockSpec((D_MODEL, 2 * D_CONV), const),
            pl.BlockSpec((D_SSM, D_MODEL), const),
        ],
        out_specs=[pl.BlockSpec((None, NC_LAT, D_CONV), lambda s: (s, 0, 0)),
                   pl.BlockSpec((N_GROUPS, SSM_GROUP, NC_LAT), lambda s: (0, s, 0))],
        scratch_shapes=[
            pltpu.VMEM((2, NC_LAT, D_MODEL), F32),
            pltpu.SemaphoreType.DMA((2,)),
        ],
        compiler_params=pltpu.CompilerParams(dimension_semantics=("arbitrary",),
                                             vmem_limit_bytes=VMEM_LIMIT),
        name="inproj",
    )(x3, mods, norm_g, w_conv, w_ssm_t)


def _ctxproj_kernel(x_ref, mod_ref, g_ref, w_ref, u_ref):
    h = _rms(x_ref[...], g_ref[...])
    h = (h * (1.0 + mod_ref[1:2, :]) + mod_ref[0:1, :]).astype(BF16)
    u_ref[...] = _dot(h, w_ref[...])


def _ctxproj(ctx2d, mods, norm_g, w_ssm):
    return pl.pallas_call(
        _ctxproj_kernel,
        out_shape=jax.ShapeDtypeStruct((CTX_LEN, D_SSM), F32),
        name="ctxproj",
    )(ctx2d, mods, norm_g, w_ssm)


def _split2(v):
    hi = v.astype(BF16)
    lo = (v - hi.astype(F32)).astype(BF16)
    return hi, lo


def _expand_right(table, sel2):
    return _dot(jnp.concatenate(_split2(table), axis=1), sel2)


def _expand_left(sel2, table):
    return _dot(sel2, jnp.concatenate(_split2(table), axis=0))


def _ssm_selectors():
    i32 = jnp.int32
    tab = jnp.arange(TAB, dtype=i32)
    s_of = jnp.arange(CW, dtype=i32) >> GROUP_SHIFT
    j_of = jnp.arange(2 * CW, dtype=i32) >> GROUP_SHIFT
    e_t = jnp.stack([s_of + 1, CHUNK - s_of])
    e_j = jnp.stack([(CHUNK - 1) - j_of, j_of - (CHUNK - 1)])
    e_j = jnp.where((e_j >= 0) & (e_j < CHUNK), e_j, -1)
    sel_sb = (tab[:, None] == s_of[None, :]).astype(BF16)
    sel_t = (e_t[:, :, None] == tab[None, None, :]).astype(BF16)
    sel_j = (tab[None, :, None] == e_j[:, None, :]).astype(BF16)
    ci_sel = (jnp.arange(SSM_GROUP, dtype=i32)[:, None]
              == (jnp.arange(CW, dtype=i32) & (SSM_GROUP - 1))[None, :]).astype(BF16)
    return (jnp.concatenate([sel_sb, sel_sb], axis=0), jnp.concatenate([sel_t, sel_t], axis=2),
            jnp.concatenate([sel_j, sel_j], axis=1), ci_sel)


def _ssm_w_kernel(arow_ref, acol_ref, ldt_ref, b_ref, c_ref, d_ref, selsb_ref, selt_ref, selj_ref,
                  cisel_ref, wxt_ref, toept_ref, wct_ref, dec_ref):
    i32 = jnp.int32
    tau_l = jnp.minimum(lax.broadcasted_iota(i32, (1, TAB), 1), CHUNK).astype(F32)
    tau_r = jnp.minimum(lax.broadcasted_iota(i32, (TAB, 1), 0), CHUNK).astype(F32)
    lane_g = lax.broadcasted_iota(i32, (1, 2 * CW), 1)
    ch_row = lax.broadcasted_iota(i32, (SSM_GROUP, 1), 0)
    ci_sel = cisel_ref[...]

    g_lhs, g_rhs = [], []
    for d in range(2):
        dt = jnp.exp(ldt_ref[d])
        ar_r, ai_r = arow_ref[d, 0:1, :], arow_ref[d, 1:2, :]
        ar_c, ai_c = acol_ref[d, :, 0:1], acol_ref[d, :, 1:2]

        mag_c = jnp.exp(ar_c * dt * tau_l)
        ph_c = ai_c * dt * tau_l
        pc_re, pc_im = mag_c * jnp.cos(ph_c), mag_c * jnp.sin(ph_c)
        mag_r = jnp.exp(tau_r * (ar_r * dt))
        ph_r = tau_r * (ai_r * dt)
        pr_re, pr_im = mag_r * jnp.cos(ph_r), mag_r * jnp.sin(ph_r)

        n_re, n_im = pc_re[:, 1:2] - 1.0, pc_im[:, 1:2]
        den = ar_c * ar_c + ai_c * ai_c
        q_re = (n_re * ar_c + n_im * ai_c) / den
        q_im = (n_im * ar_c - n_re * ai_c) / den
        b_re, b_im = b_ref[d, 0], b_ref[d, 1]
        bb_re = q_re * b_re - q_im * b_im
        bb_im = q_re * b_im + q_im * b_re
        bt_re = sum(_dot(piece, ci_sel) for piece in _split2(bb_re))
        bt_im = sum(_dot(piece, ci_sel) for piece in _split2(bb_im))

        pg_re, pg_im = _expand_right(pc_re, selj_ref[d]), _expand_right(pc_im, selj_ref[d])
        bg_re = jnp.concatenate([bt_re, bt_re], axis=1)
        bg_im = jnp.concatenate([bt_im, bt_im], axis=1)
        m_re = pg_re * bg_re - pg_im * bg_im
        m_im = pg_re * bg_im + pg_im * bg_re
        c_re, c_im = c_ref[d, 0], c_ref[d, 1]
        for cv, mv in ((c_re, m_re), (-c_im, m_im)):
            c_hi, c_lo = _split2(cv)
            m_hi, m_lo = _split2(mv)
            g_lhs += [c_hi, c_hi, c_lo]
            g_rhs += [m_hi, m_lo, m_hi]

        if d == 0:
            px_re, px_im = pg_re[:, :CW], pg_im[:, :CW]
        else:
            px_re, px_im = _expand_right(pc_re, selsb_ref[...]), _expand_right(pc_im, selsb_ref[...])
        wxt_ref[d * STATE:(d + 1) * STATE, :] = (px_re * bt_re - px_im * bt_im).astype(wxt_ref.dtype)
        wxt_ref[(2 + d) * STATE:(3 + d) * STATE, :] = (px_re * bt_im + px_im * bt_re).astype(wxt_ref.dtype)

        py_re, py_im = _expand_left(selt_ref[d], pr_re), _expand_left(selt_ref[d], pr_im)
        ct_re = jnp.broadcast_to(c_re[None], (CHUNK, SSM_GROUP, STATE)).reshape(CW, STATE)
        ct_im = jnp.broadcast_to(c_im[None], (CHUNK, SSM_GROUP, STATE)).reshape(CW, STATE)
        wct_ref[:, d * STATE:(d + 1) * STATE] = (ct_re * py_re - ct_im * py_im).astype(wct_ref.dtype)
        wct_ref[:, (2 + d) * STATE:(3 + d) * STATE] = (-(ct_re * py_im + ct_im * py_re)).astype(wct_ref.dtype)

        dec_ref[:, d * STATE:(d + 1) * STATE] = pr_re[CHUNK:CHUNK + 1, :]
        dec_ref[:, (2 + d) * STATE:(3 + d) * STATE] = pr_im[CHUNK:CHUNK + 1, :]

    g_lhs = jnp.concatenate([piece.astype(F32) for piece in g_lhs], axis=1).astype(BF16)
    g_pad = _dot(g_lhs, jnp.concatenate(g_rhs, axis=0))
    on_diag = lane_g == (CHUNK - 1) * SSM_GROUP + ch_row
    g_pad = g_pad + jnp.where(on_diag, d_ref[...], 0.0)
    for t in range(CHUNK):
        k = (CHUNK - 1 - t) * SSM_GROUP
        rolled = g_pad if k == 0 else pltpu.roll(g_pad, 2 * CW - k, axis=1)
        toept_ref[t * SSM_GROUP:(t + 1) * SSM_GROUP, :] = rolled[:, :CW].astype(toept_ref.dtype)


def _ssm_weights(arow, acol, ldt, b, c, dcol):
    def spec(*tail):
        zeros = (0,) * len(tail)
        return pl.BlockSpec((None,) + tail, lambda g: (g,) + zeros)

    def const(arr):
        zeros = (0,) * arr.ndim
        return pl.BlockSpec(arr.shape, lambda g: zeros)

    sels = _ssm_selectors()
    return pl.pallas_call(
        _ssm_w_kernel,
        out_shape=[
            jax.ShapeDtypeStruct((N_GROUPS, SW, CW), BF16),
            jax.ShapeDtypeStruct((N_GROUPS, CW, CW), BF16),
            jax.ShapeDtypeStruct((N_GROUPS, CW, SW), BF16),
            jax.ShapeDtypeStruct((N_GROUPS, 1, SW), F32),
        ],
        grid=(N_GROUPS,),
        in_specs=[spec(2, 2, STATE), spec(2, STATE, 2), spec(2, 1, 1),
                  spec(2, 2, STATE, SSM_GROUP), spec(2, 2, SSM_GROUP, STATE), spec(SSM_GROUP, 1)]
                 + [const(arr) for arr in sels],
        out_specs=[spec(SW, CW), spec(CW, CW), spec(CW, SW), spec(1, SW)],
        name="ssm_w",
    )(arow, acol, ldt, b, c, dcol, *sels)


def _ssm_x_kernel(ut_ref, utc_ref, wxt_ref, x_ref):
    half = 2 * STATE
    xt = _dot(wxt_ref[...], ut_ref[...])
    xc = _dot(wxt_ref[...], utc_ref[...])
    for tile in range(2):
        rows = slice(tile * half, (tile + 1) * half)
        x_ref[tile, 0:NC_LAT, :] = xt[rows, :].T
        x_ref[tile, NC_LAT:NC, :] = xc[rows, :].T[0:NC_CTX, :]


def _ssm_x(ut, ut_ctx, wxt):
    return pl.pallas_call(
        _ssm_x_kernel,
        out_shape=jax.ShapeDtypeStruct((2, N_GROUPS * NC, 2 * STATE), F32),
        grid=(N_GROUPS,),
        in_specs=[pl.BlockSpec((None, CW, NC_LAT), lambda g: (g, 0, 0)),
                  pl.BlockSpec((None, CW, LANES), lambda g: (g, 0, 0)),
                  pl.BlockSpec((None, SW, CW), lambda g: (g, 0, 0))],
        out_specs=pl.BlockSpec((2, NC, 2 * STATE), lambda g: (0, g, 0)),
        name="ssm_x",
    )(ut, ut_ctx, wxt)


def _ssm_sy_kernel(x_ref, dec_ref, ut_ref, toept_ref, wct_ref, yt_ref, sf_scr, sb_scr):
    half = 2 * STATE
    d_re = dec_ref[:, 0:half]
    d_im = dec_ref[:, half:2 * half]
    is_fwd = lax.broadcasted_iota(jnp.int32, (SCAN_GROUPS, half), 1) < STATE

    def rows(c):
        return pl.ds(c, SCAN_GROUPS, stride=NC)

    def body(k, carry):
        s_re, s_im = carry
        cf = jnp.where(k < NC_CTX, NC_LAT + k, k - NC_CTX)
        cb = NC - 1 - k
        sf_scr[0, rows(cf), :] = s_re
        sf_scr[1, rows(cf), :] = s_im
        sb_scr[0, rows(cb), :] = s_re
        sb_scr[1, rows(cb), :] = s_im
        x_re = jnp.where(is_fwd, x_ref[0, rows(cf), :], x_ref[0, rows(cb), :])
        x_im = jnp.where(is_fwd, x_ref[1, rows(cf), :], x_ref[1, rows(cb), :])
        n_re = d_re * s_re - d_im * s_im + x_re
        n_im = d_re * s_im + d_im * s_re + x_im
        return n_re, n_im

    zero = jnp.zeros((SCAN_GROUPS, half), F32)
    lax.fori_loop(0, NC, body, (zero, zero))

    lat_fwd = lax.broadcasted_iota(jnp.int32, (NC_LAT, half), 1) < STATE
    for g in range(SCAN_GROUPS):
        r = slice(g * NC, g * NC + NC_LAT)
        s_g = jnp.concatenate([jnp.where(lat_fwd, sf_scr[tile, r, :], sb_scr[tile, r, :])
                               for tile in range(2)], axis=1).astype(BF16)
        y = _dot(toept_ref[g], ut_ref[g]) + _dot_nt(wct_ref[g], s_g)
        yt_ref[g] = y.astype(yt_ref.dtype)


def _ssm_sy(x_st, dec, ut, toept, wct):
    per = SCAN_GROUPS
    return pl.pallas_call(
        _ssm_sy_kernel,
        out_shape=jax.ShapeDtypeStruct((N_GROUPS, CW, NC_LAT), BF16),
        grid=(N_GROUPS // per,),
        in_specs=[pl.BlockSpec((2, per * NC, 2 * STATE), lambda o: (0, o, 0)),
                  pl.BlockSpec((per, SW), lambda o: (o, 0)),
                  pl.BlockSpec((per, CW, NC_LAT), lambda o: (o, 0, 0)),
                  pl.BlockSpec((per, CW, CW), lambda o: (o, 0, 0)),
                  pl.BlockSpec((per, CW, SW), lambda o: (o, 0, 0))],
        out_specs=pl.BlockSpec((per, CW, NC_LAT), lambda o: (o, 0, 0)),
        scratch_shapes=[pltpu.VMEM((2, per * NC, 2 * STATE), F32),
                        pltpu.VMEM((2, per * NC, 2 * STATE), F32)],
        compiler_params=pltpu.CompilerParams(vmem_limit_bytes=VMEM_LIMIT),
        name="ssm_sy",
    )(x_st, dec, ut, toept, wct)


def _conv_ln_swish(a_ref, cw_ref, cb_ref, lg_ref, lb_ref, apad, ash, yc_scr):
    apad[CONV_PAD:CONV_PAD + NC_LAT, :] = a_ref[...]
    span = CONV_ROWS - SUBLANES
    for m in range(1, CONV_PHASES):
        ash[m - 1, 0:span, :] = apad[pl.ds(ROW_DIL * m, span), :]

    for rb in range(NC_LAT // CONV_BLOCK):
        r0 = rb * CONV_BLOCK
        pieces = []
        for lt in range(D_CONV // LANES):
            lanes = slice(lt * LANES, (lt + 1) * LANES)
            acc = None
            for k in range(CONV_WIDTH):
                off = CONV_PAD + ROW_DIL * (k - CONV_WIDTH // 2)
                m, q = (off // ROW_DIL) % CONV_PHASES, off // SUBLANES
                src = apad if m == 0 else ash.at[m - 1]
                r1 = r0 + SUBLANES * q
                term = src[r1:r1 + CONV_BLOCK, lanes] * cw_ref[k:k + 1, lanes]
                acc = term if acc is None else acc + term
            pieces.append(acc)
        y = jnp.concatenate(pieces, axis=1) + cb_ref[...]
        mu = jnp.mean(y, axis=-1, keepdims=True)
        yc = y - mu
        var = jnp.mean(yc * yc, axis=-1, keepdims=True)
        yn = yc * lax.rsqrt(var + EPS) * lg_ref[...] + lb_ref[...]
        yc_scr[r0:r0 + CONV_BLOCK, :] = (yn * _sigmoid(yn)).astype(yc_scr.dtype)


def _tail_kernel(x_hbm, a_ref, yt_ref, mod_ref, cw_ref, cb_ref, lg_ref, lb_ref, wglut_ref, bglu_ref,
                 wout_ref, n2g_ref, wgu_ref, wdn_ref, fg_ref, o_hbm,
                 xbuf, obuf, sem, osem, apad, ash, yc_scr, act_scr):
    t = pl.program_id(0)
    n = pl.num_programs(0)
    slot = t % 2

    @pl.when(t == 0)
    def _():
        _row_gather(x_hbm, xbuf, sem, 0, 0).start()
        zeros = jnp.zeros((CONV_PAD, D_CONV), F32)
        apad[0:CONV_PAD, :] = zeros
        apad[CONV_PAD + NC_LAT:CONV_ROWS, :] = zeros

    @pl.when(t + 1 < n)
    def _():
        _row_gather(x_hbm, xbuf, sem, t + 1, 1 - slot).start()

    @pl.when(t >= 2)
    def _():
        _row_scatter(obuf, o_hbm, osem, t - 2, slot).wait()

    _row_gather(x_hbm, xbuf, sem, t, slot).wait()

    g1 = mod_ref[2:3, :]
    sh2 = mod_ref[3:4, :]
    sc2 = mod_ref[4:5, :]
    g2 = mod_ref[5:6, :]

    _conv_ln_swish(a_ref, cw_ref, cb_ref, lg_ref, lb_ref, apad, ash, yc_scr)

    y_t = yt_ref[...].reshape(D_SSM, NC_LAT).astype(F32)
    gl = jax.nn.gelu(y_t, approximate=True)
    gate = _dot(wglut_ref[...], gl.astype(BF16)) + bglu_ref[...]
    yss_t = (gl * _sigmoid(gate)).astype(BF16)

    mix = _dot(yc_scr[...], wout_ref[0:D_CONV, :]) + _dot_tn(yss_t, wout_ref[D_CONV:, :])
    x1 = xbuf[slot] + g1 * mix
    h2 = (_rms(x1, n2g_ref[...]) * (1.0 + sc2) + sh2).astype(BF16)

    for j in range(D_FF // FF_BLOCK):
        cols = slice(j * FF_BLOCK, (j + 1) * FF_BLOCK)
        up_cols = slice(D_FF + j * FF_BLOCK, D_FF + (j + 1) * FF_BLOCK)
        gt = _dot(h2, wgu_ref[:, cols])
        up = _dot(h2, wgu_ref[:, up_cols])
        act_scr[:, cols] = (gt * _sigmoid(gt) * up).astype(act_scr.dtype)
    x2 = x1 + g2 * _dot(act_scr[...], wdn_ref[...])
    obuf[slot] = _rms(x2, fg_ref[...])

    _row_scatter(obuf, o_hbm, osem, t, slot).start()

    @pl.when(t == n - 1)
    def _():
        _row_scatter(obuf, o_hbm, osem, t - 1, 1 - slot).wait()
        _row_scatter(obuf, o_hbm, osem, t, slot).wait()


def _tail(x3, a, yt, mods, conv_w, conv_b, ln_g, ln_b, wglut, bglu, wout, n2g, wgu, wdn, fg):
    const = lambda t: (0, 0)

    def resident(shape):
        return pl.BlockSpec(shape, const, pipeline_mode=pl.Buffered(1))

    return pl.pallas_call(
        _tail_kernel,
        out_shape=jax.ShapeDtypeStruct((NC_LAT, CHUNK, D_MODEL), F32),
        grid=(CHUNK,),
        in_specs=[
            pl.BlockSpec(memory_space=pl.ANY),
            pl.BlockSpec((None, NC_LAT, D_CONV), lambda t: (t, 0, 0)),
            pl.BlockSpec((N_GROUPS, SSM_GROUP, NC_LAT), lambda t: (0, t, 0)),
            resident((N_MOD, D_MODEL)),
            resident((CONV_WIDTH, D_CONV)),
            resident((1, D_CONV)),
            resident((1, D_CONV)),
            resident((1, D_CONV)),
            resident((D_SSM, D_SSM)),
            resident((D_SSM, 1)),
            resident((D_MODEL, D_MODEL)),
            resident((1, D_MODEL)),
            resident((D_MODEL, 2 * D_FF)),
            resident((D_FF, D_MODEL)),
            resident((1, D_MODEL)),
        ],
        out_specs=pl.BlockSpec(memory_space=pl.ANY),
        scratch_shapes=[
            pltpu.VMEM((2, NC_LAT, D_MODEL), F32),
            pltpu.VMEM((2, NC_LAT, D_MODEL), F32),
            pltpu.SemaphoreType.DMA((2,)),
            pltpu.SemaphoreType.DMA((2,)),
            pltpu.VMEM((CONV_ROWS, D_CONV), F32),
            pltpu.VMEM((CONV_PHASES - 1, CONV_ROWS, D_CONV), F32),
            pltpu.VMEM((NC_LAT, D_CONV), BF16),
            pltpu.VMEM((NC_LAT, D_FF), BF16),
        ],
        compiler_params=pltpu.CompilerParams(dimension_semantics=("arbitrary",),
                                             vmem_limit_bytes=VMEM_LIMIT),
        name="tail",
    )(x3, a, yt, mods, conv_w, conv_b, ln_g, ln_b, wglut, bglu, wout, n2g, wgu, wdn, fg)


def kernel(x, c, ctx, c_ctx, w_mod, b_mod, norm1_g, w_in, conv_w, conv_b, conv_ln_g, conv_ln_b,
           ssm_a_re, ssm_a_im, ssm_log_dt, ssm_b_re, ssm_b_im, ssm_c_re, ssm_c_im, ssm_d,
           ssm_w_glu, ssm_b_glu, w_out, norm2_g, w_gate_up, w_down, final_norm_g):
    assert x.shape == (1, SEQ, D_MODEL) and ctx.shape == (1, CTX_LEN, D_MODEL)
    assert w_mod.shape[0] == 1, "single trunk layer"
    x3 = x.reshape(NC_LAT, CHUNK, D_MODEL)

    mods_all = _mod(jnp.stack([c[0], c_ctx], axis=1), w_mod[0], b_mod[0][None, :])
    mods = mods_all[0].reshape(N_MOD, D_MODEL)
    mods_ctx = mods_all[1].reshape(N_MOD, D_MODEL)

    w_in_b = w_in[0].astype(BF16)
    w_ssm = w_in_b[:, 2 * D_CONV:]
    n1g = norm1_g[0][None, :]
    a, ut = _inproj(x3, mods, n1g, w_in_b[:, :2 * D_CONV], w_ssm.T)
    u_ctx = _ctxproj(ctx[0], mods_ctx, n1g, w_ssm)
    ut_ctx = u_ctx.reshape(NC_CTX, CHUNK, N_GROUPS, SSM_GROUP).transpose(2, 1, 3, 0)
    ut_ctx = jnp.pad(ut_ctx.reshape(N_GROUPS, CW, NC_CTX).astype(BF16),
                     ((0, 0), (0, 0), (0, LANES - NC_CTX)))

    a_re, a_im = ssm_a_re[0], ssm_a_im[0]
    arow = jnp.stack([a_re, a_im], axis=2).transpose(1, 0, 2, 3)
    acol = jnp.stack([a_re, a_im], axis=3).transpose(1, 0, 2, 3)
    ldt = ssm_log_dt[0].transpose(1, 0)[:, :, None, None]
    b_p = jnp.stack([ssm_b_re[0], ssm_b_im[0]], axis=2).transpose(1, 0, 2, 3, 4)
    c_p = jnp.stack([ssm_c_re[0], ssm_c_im[0]], axis=2).transpose(1, 0, 2, 3, 4)
    dcol = ssm_d[0].reshape(N_GROUPS, SSM_GROUP, 1)
    wxt, toept, wct, dec = _ssm_weights(arow, acol, ldt, b_p, c_p, dcol)

    x_st = _ssm_x(ut, ut_ctx, wxt)
    yt = _ssm_sy(x_st, dec.reshape(N_GROUPS, SW), ut, toept, wct)

    out3 = _tail(x3, a, yt, mods, conv_w[0], conv_b[0][None, :], conv_ln_g[0][None, :],
                 conv_ln_b[0][None, :], ssm_w_glu[0].T.astype(BF16), ssm_b_glu[0][:, None],
                 w_out[0].astype(BF16), norm2_g[0][None, :], w_gate_up[0].astype(BF16),
                 w_down[0].astype(BF16), final_norm_g[None, :])
    return out3.reshape(1, SEQ, D_MODEL)
```

```python
import jax
import jax.numpy as jnp
from jax import lax
from jax.experimental import pallas as pl
from jax.experimental.pallas import tpu as pltpu

F32 = jnp.float32
BF16 = jnp.bfloat16

D_MODEL = 1024
SEQ = 16384
GRID_W = 64
CTX_LEN = 256
D_CONV = 512
D_SSM = 512
SSM_GROUP = 16
GROUP_SHIFT = 4
N_GROUPS = 32
STATE = 64
CONV_WIDTH = 31
D_FF = 2816
N_MOD = 6
EPS = 1e-6

CHUNK = 32
NC_LAT = SEQ // CHUNK
NC_CTX = CTX_LEN // CHUNK
NC = NC_LAT + NC_CTX
CW = CHUNK * SSM_GROUP
SW = 4 * STATE
TAB = 128
LANES = 128
SUBLANES = 8

ROW_DIL = GRID_W // CHUNK
CONV_PAD = 32
CONV_ROWS = NC_LAT + 2 * CONV_PAD
CONV_PHASES = SUBLANES // ROW_DIL
CONV_BLOCK = 64
SCAN_GROUPS = 8
FF_BLOCK = 256
VMEM_LIMIT = 56 * 1024 * 1024


def _sigmoid(v):
    return 1.0 / (1.0 + jnp.exp(-v))


def _rms(v, g):
    ms = jnp.mean(v * v, axis=-1, keepdims=True)
    return v * lax.rsqrt(ms + EPS) * g


def _dot(a, b):
    return jnp.dot(a, b, preferred_element_type=F32)


def _dot_nt(a, b):
    return lax.dot_general(a, b, (((1,), (1,)), ((), ())), preferred_element_type=F32)


def _row_gather(x_hbm, buf, sem, step, slot):
    return pltpu.make_async_copy(x_hbm.at[:, step, :], buf.at[slot], sem.at[slot])


def _row_scatter(buf, o_hbm, sem, step, slot):
    return pltpu.make_async_copy(buf.at[slot], o_hbm.at[:, step, :], sem.at[slot])


def _mod_kernel(c_ref, w_ref, b_ref, o_ref):
    c = c_ref[...]
    s = c * _sigmoid(c)
    w = w_ref[...]
    for v in range(2):
        o_ref[v:v + 1, :] = jnp.sum(s[:, v:v + 1] * w, axis=0, keepdims=True) + b_ref[...]


def _mod(ccols, w_mod, b_mod):
    return pl.pallas_call(
        _mod_kernel,
        out_shape=jax.ShapeDtypeStruct((2, N_MOD * D_MODEL), F32),
        grid=(N_MOD,),
        in_specs=[
            pl.BlockSpec((D_MODEL, 2), lambda j: (0, 0)),
            pl.BlockSpec((D_MODEL, D_MODEL), lambda j: (0, j)),
            pl.BlockSpec((1, D_MODEL), lambda j: (0, j)),
        ],
        out_specs=pl.BlockSpec((2, D_MODEL), lambda j: (0, j)),
        name="mod",
    )(ccols, w_mod, b_mod)


def _inproj_kernel(x_hbm, mod_ref, g_ref, wc_ref, wst_ref, a_ref, ut_ref, xbuf, sem):
    s = pl.program_id(0)
    n = pl.num_programs(0)
    slot = s % 2

    @pl.when(s == 0)
    def _():
        _row_gather(x_hbm, xbuf, sem, 0, 0).start()

    @pl.when(s + 1 < n)
    def _():
        _row_gather(x_hbm, xbuf, sem, s + 1, 1 - slot).start()

    _row_gather(x_hbm, xbuf, sem, s, slot).wait()

    h = _rms(xbuf[slot], g_ref[...])
    h = (h * (1.0 + mod_ref[1:2, :]) + mod_ref[0:1, :]).astype(BF16)
    z = _dot(h, wc_ref[...])
    a_ref[...] = z[:, :D_CONV] * _sigmoid(z[:, D_CONV:])
    zt = _dot_nt(wst_ref[...], h)
    ut_ref[...] = zt.reshape(N_GROUPS, SSM_GROUP, NC_LAT).astype(ut_ref.dtype)


def _inproj(x3, mods, norm_g, w_conv, w_ssm_t):
    const = lambda s: (0, 0)
    return pl.pallas_call(
        _inproj_kernel,
        out_shape=[jax.ShapeDtypeStruct((CHUNK, NC_LAT, D_CONV), F32),
                   jax.ShapeDtypeStruct((N_GROUPS, CW, NC_LAT), BF16)],
        grid=(CHUNK,),
        in_specs=[
            pl.BlockSpec(memory_space=pl.ANY),
            pl.BlockSpec((N_MOD, D_MODEL), const),
            pl.BlockSpec((1, D_MODEL), const),
            pl.BlockSpec((D_MODEL, 2 * D_CONV), const),
            pl.BlockSpec((D_SSM, D_MODEL), const),
        ],
        out_specs=[pl.BlockSpec((None, NC_LAT, D_CONV), lambda s: (s, 0, 0)),
                   pl.BlockSpec((N_GROUPS, SSM_GROUP, NC_LAT), lambda s: (0, s, 0))],
        scratch_shapes=[
            pltpu.VMEM((2, NC_LAT, D_MODEL), F32),
            pltpu.SemaphoreType.DMA((2,)),
        ],
        compiler_params=pltpu.CompilerParams(dimension_semantics=("arbitrary",),
                                             vmem_limit_bytes=VMEM_LIMIT),
        name="inproj",
    )(x3, mods, norm_g, w_conv, w_ssm_t)


def _ctxproj_kernel(x_ref, mod_ref, g_ref, w_ref, u_ref):
    h = _rms(x_ref[...], g_ref[...])
    h = (h * (1.0 + mod_ref[1:2, :]) + mod_ref[0:1, :]).astype(BF16)
    u_ref[...] = _dot(h, w_ref[...])


def _ctxproj(ctx2d, mods, norm_g, w_ssm):
    return pl.pallas_call(
        _ctxproj_kernel,
        out_shape=jax.ShapeDtypeStruct((CTX_LEN, D_SSM), F32),
        name="ctxproj",
    )(ctx2d, mods, norm_g, w_ssm)


def _split2(v):
    hi = v.astype(BF16)
    lo = (v - hi.astype(F32)).astype(BF16)
    return hi, lo


def _expand_right(table, sel2):
    return _dot(jnp.concatenate(_split2(table), axis=1), sel2)


def _expand_left(sel2, table):
    return _dot(sel2, jnp.concatenate(_split2(table), axis=0))


def _ssm_selectors():
    i32 = jnp.int32
    tab = jnp.arange(TAB, dtype=i32)
    s_of = jnp.arange(CW, dtype=i32) >> GROUP_SHIFT
    j_of = jnp.arange(2 * CW, dtype=i32) >> GROUP_SHIFT
    e_t = jnp.stack([s_of + 1, CHUNK - s_of])
    e_j = jnp.stack([(CHUNK - 1) - j_of, j_of - (CHUNK - 1)])
    e_j = jnp.where((e_j >= 0) & (e_j < CHUNK), e_j, -1)
    sel_sb = (tab[:, None] == s_of[None, :]).astype(BF16)
    sel_t = (e_t[:, :, None] == tab[None, None, :]).astype(BF16)
    sel_j = (tab[None, :, None] == e_j[:, None, :]).astype(BF16)
    ci_sel = (jnp.arange(SSM_GROUP, dtype=i32)[:, None]
              == (jnp.arange(CW, dtype=i32) & (SSM_GROUP - 1))[None, :]).astype(BF16)
    return (jnp.concatenate([sel_sb, sel_sb], axis=0), jnp.concatenate([sel_t, sel_t], axis=2),
            jnp.concatenate([sel_j, sel_j], axis=1), ci_sel)


def _ssm_w_kernel(arow_ref, acol_ref, ldt_ref, b_ref, c_ref, d_ref, selsb_ref, selt_ref, selj_ref,
                  cisel_ref, wxt_ref, toept_ref, wct_ref, dec_ref):
    i32 = jnp.int32
    tau_l = jnp.minimum(lax.broadcasted_iota(i32, (1, TAB), 1), CHUNK).astype(F32)
    tau_r = jnp.minimum(lax.broadcasted_iota(i32, (TAB, 1), 0), CHUNK).astype(F32)
    lane_g = lax.broadcasted_iota(i32, (1, 2 * CW), 1)
    ch_row = lax.broadcasted_iota(i32, (SSM_GROUP, 1), 0)
    ci_sel = cisel_ref[...]

    g_lhs, g_rhs = [], []
    for d in range(2):
        dt = jnp.exp(ldt_ref[d])
        ar_r, ai_r = arow_ref[d, 0:1, :], arow_ref[d, 1:2, :]
        ar_c, ai_c = acol_ref[d, :, 0:1], acol_ref[d, :, 1:2]

        mag_c = jnp.exp(ar_c * dt * tau_l)
        ph_c = ai_c * dt * tau_l
        pc_re, pc_im = mag_c * jnp.cos(ph_c), mag_c * jnp.sin(ph_c)
        mag_r = jnp.exp(tau_r * (ar_r * dt))
        ph_r = tau_r * (ai_r * dt)
        pr_re, pr_im = mag_r * jnp.cos(ph_r), mag_r * jnp.sin(ph_r)

        n_re, n_im = pc_re[:, 1:2] - 1.0, pc_im[:, 1:2]
        den = ar_c * ar_c + ai_c * ai_c
        q_re = (n_re * ar_c + n_im * ai_c) / den
        q_im = (n_im * ar_c - n_re * ai_c) / den
        b_re, b_im = b_ref[d, 0], b_ref[d, 1]
        bb_re = q_re * b_re - q_im * b_im
        bb_im = q_re * b_im + q_im * b_re
        bt_re = sum(_dot(piece, ci_sel) for piece in _split2(bb_re))
        bt_im = sum(_dot(piece, ci_sel) for piece in _split2(bb_im))

        pg_re, pg_im = _expand_right(pc_re, selj_ref[d]), _expand_right(pc_im, selj_ref[d])
        bg_re = jnp.concatenate([bt_re, bt_re], axis=1)
        bg_im = jnp.concatenate([bt_im, bt_im], axis=1)
        m_re = pg_re * bg_re - pg_im * bg_im
        m_im = pg_re * bg_im + pg_im * bg_re
        c_re, c_im = c_ref[d, 0], c_ref[d, 1]
        for cv, mv in ((c_re, m_re), (-c_im, m_im)):
            c_hi, c_lo = _split2(cv)
            m_hi, m_lo = _split2(mv)
            g_lhs += [c_hi, c_hi, c_lo]
            g_rhs += [m_hi, m_lo, m_hi]

        if d == 0:
            px_re, px_im = pg_re[:, :CW], pg_im[:, :CW]
        else:
            px_re, px_im = _expand_right(pc_re, selsb_ref[...]), _expand_right(pc_im, selsb_ref[...])
        wxt_ref[d * STATE:(d + 1) * STATE, :] = (px_re * bt_re - px_im * bt_im).astype(wxt_ref.dtype)
        wxt_ref[(2 + d) * STATE:(3 + d) * STATE, :] = (px_re * bt_im + px_im * bt_re).astype(wxt_ref.dtype)

        py_re, py_im = _expand_left(selt_ref[d], pr_re), _expand_left(selt_ref[d], pr_im)
        ct_re = jnp.broadcast_to(c_re[None], (CHUNK, SSM_GROUP, STATE)).reshape(CW, STATE)
        ct_im = jnp.broadcast_to(c_im[None], (CHUNK, SSM_GROUP, STATE)).reshape(CW, STATE)
        wct_ref[:, d * STATE:(d + 1) * STATE] = (ct_re * py_re - ct_im * py_im).astype(wct_ref.dtype)
        wct_ref[:, (2 + d) * STATE:(3 + d) * STATE] = (-(ct_re * py_im + ct_im * py_re)).astype(wct_ref.dtype)

        dec_ref[:, d * STATE:(d + 1) * STATE] = pr_re[CHUNK:CHUNK + 1, :]
        dec_ref[:, (2 + d) * STATE:(3 + d) * STATE] = pr_im[CHUNK:CHUNK + 1, :]

    g_lhs = jnp.concatenate([piece.astype(F32) for piece in g_lhs], axis=1).astype(BF16)
    g_pad = _dot(g_lhs, jnp.concatenate(g_rhs, axis=0))
    on_diag = lane_g == (CHUNK - 1) * SSM_GROUP + ch_row
    g_pad = g_pad + jnp.where(on_diag, d_ref[...], 0.0)
    for t in range(CHUNK):
        k = (CHUNK - 1 - t) * SSM_GROUP
        rolled = g_pad if k == 0 else pltpu.roll(g_pad, 2 * CW - k, axis=1)
        toept_ref[t * SSM_GROUP:(t + 1) * SSM_GROUP, :] = rolled[:, :CW].astype(toept_ref.dtype)


def _ssm_weights(arow, acol, ldt, b, c, dcol):
    def spec(*tail):
        zeros = (0,) * len(tail)
        return pl.BlockSpec((None,) + tail, lambda g: (g,) + zeros)

    def const(arr):
        zeros = (0,) * arr.ndim
        return pl.BlockSpec(arr.shape, lambda g: zeros)

    sels = _ssm_selectors()
    return pl.pallas_call(
        _ssm_w_kernel,
        out_shape=[
            jax.ShapeDtypeStruct((N_GROUPS, SW, CW), BF16),
            jax.ShapeDtypeStruct((N_GROUPS, CW, CW), BF16),
            jax.ShapeDtypeStruct((N_GROUPS, CW, SW), BF16),
            jax.ShapeDtypeStruct((N_GROUPS, 1, SW), F32),
        ],
        grid=(N_GROUPS,),
        in_specs=[spec(2, 2, STATE), spec(2, STATE, 2), spec(2, 1, 1),
                  spec(2, 2, STATE, SSM_GROUP), spec(2, 2, SSM_GROUP, STATE), spec(SSM_GROUP, 1)]
                 + [const(arr) for arr in sels],
        out_specs=[spec(SW, CW), spec(CW, CW), spec(CW, SW), spec(1, SW)],
        name="ssm_w",
    )(arow, acol, ldt, b, c, dcol, *sels)


def _ssm_kernel(ut_ref, utc_ref, wxt_ref, dec_ref, toept_ref, wct_ref, yt_ref, x_scr, sf_scr, sb_scr):
    half = 2 * STATE
    for g in range(SCAN_GROUPS):
        xt = _dot(wxt_ref[g], ut_ref[g])
        xc = _dot(wxt_ref[g], utc_ref[g])
        for tile in range(2):
            st = slice(tile * half, (tile + 1) * half)
            x_scr[tile, g * NC:g * NC + NC_LAT, :] = xt[st, :].T
            x_scr[tile, g * NC + NC_LAT:(g + 1) * NC, :] = xc[st, :].T[0:NC_CTX, :]

    d_re = dec_ref[:, 0:half]
    d_im = dec_ref[:, half:2 * half]
    is_fwd = lax.broadcasted_iota(jnp.int32, (SCAN_GROUPS, half), 1) < STATE

    def rows(c):
        return pl.ds(c, SCAN_GROUPS, stride=NC)

    def body(k, carry):
        s_re, s_im = carry
        cf = jnp.where(k < NC_CTX, NC_LAT + k, k - NC_CTX)
        cb = NC - 1 - k
        sf_scr[0, rows(cf), :] = s_re
        sf_scr[1, rows(cf), :] = s_im
        sb_scr[0, rows(cb), :] = s_re
        sb_scr[1, rows(cb), :] = s_im
        x_re = jnp.where(is_fwd, x_scr[0, rows(cf), :], x_scr[0, rows(cb), :])
        x_im = jnp.where(is_fwd, x_scr[1, rows(cf), :], x_scr[1, rows(cb), :])
        n_re = d_re * s_re - d_im * s_im + x_re
        n_im = d_re * s_im + d_im * s_re + x_im
        return n_re, n_im

    zero = jnp.zeros((SCAN_GROUPS, half), F32)
    lax.fori_loop(0, NC, body, (zero, zero))

    lat_fwd = lax.broadcasted_iota(jnp.int32, (NC_LAT, half), 1) < STATE
    for g in range(SCAN_GROUPS):
        r = slice(g * NC, g * NC + NC_LAT)
        s_g = jnp.concatenate([jnp.where(lat_fwd, sf_scr[tile, r, :], sb_scr[tile, r, :])
                               for tile in range(2)], axis=1).astype(BF16)
        y = _dot(toept_ref[g], ut_ref[g]) + _dot_nt(wct_ref[g], s_g)
        yt_ref[g] = y.astype(yt_ref.dtype)


def _ssm(ut, ut_ctx, wxt, dec, toept, wct):
    per = SCAN_GROUPS
    grp = lambda o: (o, 0, 0)
    slab = pltpu.VMEM((2, per * NC, 2 * STATE), F32)
    return pl.pallas_call(
        _ssm_kernel,
        out_shape=jax.ShapeDtypeStruct((N_GROUPS, CW, NC_LAT), BF16),
        grid=(N_GROUPS // per,),
        in_specs=[pl.BlockSpec((per, CW, NC_LAT), grp),
                  pl.BlockSpec((per, CW, LANES), grp),
                  pl.BlockSpec((per, SW, CW), grp),
                  pl.BlockSpec((per, SW), lambda o: (o, 0)),
                  pl.BlockSpec((per, CW, CW), grp),
                  pl.BlockSpec((per, CW, SW), grp)],
        out_specs=pl.BlockSpec((per, CW, NC_LAT), grp),
        scratch_shapes=[slab, slab, slab],
        compiler_params=pltpu.CompilerParams(vmem_limit_bytes=VMEM_LIMIT),
        name="ssm",
    )(ut, ut_ctx, wxt, dec, toept, wct)


def _conv_ln_swish(a_ref, cw_ref, cb_ref, lg_ref, lb_ref, apad, ash, yc_scr):
    apad[CONV_PAD:CONV_PAD + NC_LAT, :] = a_ref[...]
    span = CONV_ROWS - SUBLANES
    for m in range(1, CONV_PHASES):
        ash[m - 1, 0:span, :] = apad[pl.ds(ROW_DIL * m, span), :]

    for rb in range(NC_LAT // CONV_BLOCK):
        r0 = rb * CONV_BLOCK
        pieces = []
        for lt in range(D_CONV // LANES):
            lanes = slice(lt * LANES, (lt + 1) * LANES)
            acc = None
            for k in range(CONV_WIDTH):
                off = CONV_PAD + ROW_DIL * (k - CONV_WIDTH // 2)
                m, q = (off // ROW_DIL) % CONV_PHASES, off // SUBLANES
                src = apad if m == 0 else ash.at[m - 1]
                r1 = r0 + SUBLANES * q
                term = src[r1:r1 + CONV_BLOCK, lanes] * cw_ref[k:k + 1, lanes]
                acc = term if acc is None else acc + term
            pieces.append(acc)
        y = jnp.concatenate(pieces, axis=1) + cb_ref[...]
        mu = jnp.mean(y, axis=-1, keepdims=True)
        yc = y - mu
        var = jnp.mean(yc * yc, axis=-1, keepdims=True)
        yn = yc * lax.rsqrt(var + EPS) * lg_ref[...] + lb_ref[...]
        yc_scr[r0:r0 + CONV_BLOCK, :] = (yn * _sigmoid(yn)).astype(yc_scr.dtype)


def _tail_kernel(x_hbm, a_ref, yt_ref, mod_ref, cw_ref, cb_ref, lg_ref, lb_ref, wglu_ref, bglu_ref,
                 wout_ref, n2g_ref, wgu_ref, wdn_ref, fg_ref, o_hbm,
                 xbuf, obuf, sem, osem, apad, ash, yc_scr, act_scr):
    t = pl.program_id(0)
    n = pl.num_programs(0)
    slot = t % 2

    @pl.when(t == 0)
    def _():
        _row_gather(x_hbm, xbuf, sem, 0, 0).start()
        zeros = jnp.zeros((CONV_PAD, D_CONV), F32)
        apad[0:CONV_PAD, :] = zeros
        apad[CONV_PAD + NC_LAT:CONV_ROWS, :] = zeros

    @pl.when(t + 1 < n)
    def _():
        _row_gather(x_hbm, xbuf, sem, t + 1, 1 - slot).start()

    @pl.when(t >= 2)
    def _():
        _row_scatter(obuf, o_hbm, osem, t - 2, slot).wait()

    _row_gather(x_hbm, xbuf, sem, t, slot).wait()

    g1 = mod_ref[2:3, :]
    sh2 = mod_ref[3:4, :]
    sc2 = mod_ref[4:5, :]
    g2 = mod_ref[5:6, :]

    _conv_ln_swish(a_ref, cw_ref, cb_ref, lg_ref, lb_ref, apad, ash, yc_scr)

    y = yt_ref[...].reshape(D_SSM, NC_LAT).T.astype(F32)
    gl = jax.nn.gelu(y, approximate=True)
    gate = _dot(gl.astype(BF16), wglu_ref[...]) + bglu_ref[...]
    y_ssm = (gl * _sigmoid(gate)).astype(BF16)

    mix = _dot(yc_scr[...], wout_ref[0:D_CONV, :]) + _dot(y_ssm, wout_ref[D_CONV:, :])
    x1 = xbuf[slot] + g1 * mix
    h2 = (_rms(x1, n2g_ref[...]) * (1.0 + sc2) + sh2).astype(BF16)

    for j in range(D_FF // FF_BLOCK):
        cols = slice(j * FF_BLOCK, (j + 1) * FF_BLOCK)
        up_cols = slice(D_FF + j * FF_BLOCK, D_FF + (j + 1) * FF_BLOCK)
        gt = _dot(h2, wgu_ref[:, cols])
        up = _dot(h2, wgu_ref[:, up_cols])
        act_scr[:, cols] = (gt * _sigmoid(gt) * up).astype(act_scr.dtype)
    x2 = x1 + g2 * _dot(act_scr[...], wdn_ref[...])
    obuf[slot] = _rms(x2, fg_ref[...])

    _row_scatter(obuf, o_hbm, osem, t, slot).start()

    @pl.when(t == n - 1)
    def _():
        _row_scatter(obuf, o_hbm, osem, t - 1, 1 - slot).wait()
        _row_scatter(obuf, o_hbm, osem, t, slot).wait()


def _tail(x3, a, yt, mods, conv_w, conv_b, ln_g, ln_b, wglu, bglu, wout, n2g, wgu, wdn, fg):
    const = lambda t: (0, 0)

    def resident(shape):
        return pl.BlockSpec(shape, const, pipeline_mode=pl.Buffered(1))

    return pl.pallas_call(
        _tail_kernel,
        out_shape=jax.ShapeDtypeStruct((NC_LAT, CHUNK, D_MODEL), F32),
        grid=(CHUNK,),
        in_specs=[
            pl.BlockSpec(memory_space=pl.ANY),
            pl.BlockSpec((None, NC_LAT, D_CONV), lambda t: (t, 0, 0)),
            pl.BlockSpec((N_GROUPS, SSM_GROUP, NC_LAT), lambda t: (0, t, 0)),
            resident((N_MOD, D_MODEL)),
            resident((CONV_WIDTH, D_CONV)),
            resident((1, D_CONV)),
            resident((1, D_CONV)),
            resident((1, D_CONV)),
            resident((D_SSM, D_SSM)),
            resident((1, D_SSM)),
            resident((D_MODEL, D_MODEL)),
            resident((1, D_MODEL)),
            resident((D_MODEL, 2 * D_FF)),
            resident((D_FF, D_MODEL)),
            resident((1, D_MODEL)),
        ],
        out_specs=pl.BlockSpec(memory_space=pl.ANY),
        scratch_shapes=[
            pltpu.VMEM((2, NC_LAT, D_MODEL), F32),
            pltpu.VMEM((2, NC_LAT, D_MODEL), F32),
            pltpu.SemaphoreType.DMA((2,)),
            pltpu.SemaphoreType.DMA((2,)),
            pltpu.VMEM((CONV_ROWS, D_CONV), F32),
            pltpu.VMEM((CONV_PHASES - 1, CONV_ROWS, D_CONV), F32),
            pltpu.VMEM((NC_LAT, D_CONV), BF16),
            pltpu.VMEM((NC_LAT, D_FF), BF16),
        ],
        compiler_params=pltpu.CompilerParams(dimension_semantics=("arbitrary",),
                                             vmem_limit_bytes=VMEM_LIMIT),
        name="tail",
    )(x3, a, yt, mods, conv_w, conv_b, ln_g, ln_b, wglu, bglu, wout, n2g, wgu, wdn, fg)


def kernel(x, c, ctx, c_ctx, w_mod, b_mod, norm1_g, w_in, conv_w, conv_b, conv_ln_g, conv_ln_b,
           ssm_a_re, ssm_a_im, ssm_log_dt, ssm_b_re, ssm_b_im, ssm_c_re, ssm_c_im, ssm_d,
           ssm_w_glu, ssm_b_glu, w_out, norm2_g, w_gate_up, w_down, final_norm_g):
    assert x.shape == (1, SEQ, D_MODEL) and ctx.shape == (1, CTX_LEN, D_MODEL)
    assert w_mod.shape[0] == 1, "single trunk layer"
    x3 = x.reshape(NC_LAT, CHUNK, D_MODEL)

    mods_all = _mod(jnp.stack([c[0], c_ctx], axis=1), w_mod[0], b_mod[0][None, :])
    mods = mods_all[0].reshape(N_MOD, D_MODEL)
    mods_ctx = mods_all[1].reshape(N_MOD, D_MODEL)

    w_in_b = w_in[0].astype(BF16)
    w_ssm = w_in_b[:, 2 * D_CONV:]
    n1g = norm1_g[0][None, :]
    a, ut = _inproj(x3, mods, n1g, w_in_b[:, :2 * D_CONV], w_ssm.T)
    u_ctx = _ctxproj(ctx[0], mods_ctx, n1g, w_ssm)
    ut_ctx = u_ctx.reshape(NC_CTX, CHUNK, N_GROUPS, SSM_GROUP).transpose(2, 1, 3, 0)
    ut_ctx = jnp.pad(ut_ctx.reshape(N_GROUPS, CW, NC_CTX).astype(BF16),
                     ((0, 0), (0, 0), (0, LANES - NC_CTX)))

    a_re, a_im = ssm_a_re[0], ssm_a_im[0]
    arow = jnp.stack([a_re, a_im], axis=2).transpose(1, 0, 2, 3)
    acol = jnp.stack([a_re, a_im], axis=3).transpose(1, 0, 2, 3)
    ldt = ssm_log_dt[0].transpose(1, 0)[:, :, None, None]
    b_p = jnp.stack([ssm_b_re[0], ssm_b_im[0]], axis=2).transpose(1, 0, 2, 3, 4)
    c_p = jnp.stack([ssm_c_re[0], ssm_c_im[0]], axis=2).transpose(1, 0, 2, 3, 4)
    dcol = ssm_d[0].reshape(N_GROUPS, SSM_GROUP, 1)
    wxt, toept, wct, dec = _ssm_weights(arow, acol, ldt, b_p, c_p, dcol)

    yt = _ssm(ut, ut_ctx, wxt, dec.reshape(N_GROUPS, SW), toept, wct)

    out3 = _tail(x3, a, yt, mods, conv_w[0], conv_b[0][None, :], conv_ln_g[0][None, :],
                 conv_ln_b[0][None, :], ssm_w_glu[0].astype(BF16), ssm_b_glu[0][None, :],
                 w_out[0].astype(BF16), norm2_g[0][None, :], w_gate_up[0].astype(BF16),
                 w_down[0].astype(BF16), final_norm_g[None, :])
    return out3.reshape(1, SEQ, D_MODEL)
```
